```python
import jax
import jax.numpy as jnp
from jax import lax
import numpy as np

D_MODEL = 1024
BATCH = 8
SEQ = 4096
DEPTH = 2

GRID_W = 64
CTX_LEN = 256
HEAD_DIM = 64
ROPE_THETA = 10000.0
EPS = 1e-6
Q_BLOCK = 128
NEG_BIG = -1e30

A_HEADS = 6
A_KV_HEADS = 2
B_HEADS = 4
B_Q_RANK = 256
B_KV_RANK = 128
B_NOPE = 64
B_ROPE = 32
B_V = 64
C_HEADS = 6
C_KV_HEADS = 2
C_WINDOW = 128

A_COLS = (A_HEADS + 2 * A_KV_HEADS) * HEAD_DIM
B_COLS = B_Q_RANK + B_KV_RANK + B_ROPE
C_COLS = (C_HEADS + 2 * C_KV_HEADS) * HEAD_DIM
IN_COLS = A_COLS + B_COLS + C_COLS
MIX_WIDTH = A_HEADS * HEAD_DIM + B_HEADS * B_V + C_HEADS * HEAD_DIM

D_FF_DENSE = 2816
N_EXPERTS = 8
TOP_K = 2
D_FF_EXPERT = 3584
N_DENSE = (DEPTH + 1) // 2
N_MOE = DEPTH // 2

kernel_name = "hybrid_parallel_heads_diffusion_block"


def rmsnorm(x, g):
    xf = x.astype(jnp.float32)
    y = xf * lax.rsqrt(jnp.mean(xf * xf, axis=-1, keepdims=True) + EPS)
    return (y * g.astype(jnp.float32)).astype(x.dtype)


def axial_rope_tables(n_tok, rot_dim, dtype):
    rows = n_tok // GRID_W
    row, col = jnp.meshgrid(jnp.arange(rows), jnp.arange(GRID_W), indexing="ij")
    row = row.reshape(-1).astype(jnp.float32)
    col = col.reshape(-1).astype(jnp.float32)
    n_freq = rot_dim // 4
    inv_freq = ROPE_THETA ** (-jnp.arange(n_freq, dtype=jnp.float32) / n_freq)
    ang_r = row[:, None] * inv_freq
    ang_c = col[:, None] * inv_freq
    ang = jnp.concatenate([ang_r, ang_r, ang_c, ang_c], axis=-1)
    return jnp.cos(ang).astype(dtype), jnp.sin(ang).astype(dtype)


def apply_axial_rope(t, cos, sin):
    n_freq = t.shape[-1] // 4
    t4 = t.reshape(t.shape[:-1] + (2, 2, n_freq))
    rot = jnp.concatenate([-t4[..., 1:, :], t4[..., :1, :]], axis=-2).reshape(t.shape)
    return t * cos[:, None, :] + rot * sin[:, None, :]


def rope_latent(t, n_ctx, cos, sin):
    return jnp.concatenate([t[:, :n_ctx], apply_axial_rope(t[:, n_ctx:], cos, sin)], axis=1)


def to_groups(q, n_kv):
    b, t, h, d = q.shape
    return q.reshape(b, t, n_kv, h // n_kv, d).transpose(0, 2, 3, 1, 4)


def to_heads(k):
    return k.transpose(0, 2, 1, 3)


def merge_heads(o):
    b, hk, g, t, d = o.shape
    return o.transpose(0, 3, 1, 2, 4).reshape(b, t, hk * g * d)


def sdpa(q, k, v, scale):
    s = jnp.einsum("bhgqd,bhkd->bhgqk", q, k, preferred_element_type=jnp.float32) * scale
    p = jax.nn.softmax(s, axis=-1)
    return jnp.einsum("bhgqk,bhkd->bhgqd", p.astype(v.dtype), v)


def block_sweep_attention(q, k, v, scale):
    b, hk, g, s, dk = q.shape
    nb = s // Q_BLOCK
    qb = jnp.moveaxis(q.reshape(b, hk, g, nb, Q_BLOCK, dk), 3, 0)
    ob = lax.map(lambda qi: sdpa(qi, k, v, scale), qb)
    return jnp.moveaxis(ob, 0, 3).reshape(b, hk, g, s, v.shape[-1])


def sink_softmax(scores, sink):
    m = sink
    for s in scores:
        m = jnp.maximum(m, jnp.max(s, axis=-1, keepdims=True))
    e = [jnp.exp(s - m) for s in scores]
    denom = jnp.exp(sink - m)
    for ei in e:
        denom = denom + jnp.sum(ei, axis=-1, keepdims=True)
    return [ei / denom for ei in e]


def global_gqa(qkv, n_ctx, q_gain, k_gain, cos, sin, with_ctx):
    b, t, _ = qkv.shape
    q, k, v = jnp.split(qkv, [A_HEADS * HEAD_DIM, (A_HEADS + A_KV_HEADS) * HEAD_DIM], axis=-1)
    q = rope_latent(rmsnorm(q.reshape(b, t, A_HEADS, HEAD_DIM), q_gain), n_ctx, cos, sin)
    k = rope_latent(rmsnorm(k.reshape(b, t, A_KV_HEADS, HEAD_DIM), k_gain), n_ctx, cos, sin)
    v = v.reshape(b, t, A_KV_HEADS, HEAD_DIM)
    qg, kh, vh = to_groups(q, A_KV_HEADS), to_heads(k), to_heads(v)
    scale = HEAD_DIM ** -0.5
    out_lat = merge_heads(block_sweep_attention(qg[:, :, :, n_ctx:], kh, vh, scale))
    out_ctx = None
    if with_ctx:
        out_ctx = merge_heads(sdpa(qg[:, :, :, :n_ctx], kh[:, :, :n_ctx], vh[:, :, :n_ctx], scale))
    return out_ctx, out_lat


def latent_attention(proj, n_ctx, q_gain, kv_gain, w_uq, w_ukv, cos, sin, with_ctx):
    b, t, _ = proj.shape
    cq, ckv, k_rope = jnp.split(proj, [B_Q_RANK, B_Q_RANK + B_KV_RANK], axis=-1)
    q = (rmsnorm(cq, q_gain) @ w_uq).reshape(b, t, B_HEADS, B_NOPE + B_ROPE)
    kv = (rmsnorm(ckv, kv_gain) @ w_ukv).reshape(b, t, B_HEADS, B_NOPE + B_V)
    q_nope, q_rope = jnp.split(q, [B_NOPE], axis=-1)
    k_nope, v = jnp.split(kv, [B_NOPE], axis=-1)
    q_rope = rope_latent(q_rope, n_ctx, cos, sin)
    k_rope = rope_latent(k_rope[:, :, None, :], n_ctx, cos, sin)
    q = jnp.concatenate([q_nope, q_rope], axis=-1)
    k = jnp.concatenate([k_nope, jnp.broadcast_to(k_rope, (b, t, B_HEADS, B_ROPE))], axis=-1)
    qg, kh, vh = to_groups(q, B_HEADS), to_heads(k), to_heads(v)
    scale = (B_NOPE + B_ROPE) ** -0.5
    out_lat = merge_heads(block_sweep_attention(qg[:, :, :, n_ctx:], kh, vh, scale))
    out_ctx = None
    if with_ctx:
        out_ctx = merge_heads(sdpa(qg[:, :, :, :n_ctx], kh[:, :, :n_ctx], vh[:, :, :n_ctx], scale))
    return out_ctx, out_lat


def window_sink_gqa(qkv, n_ctx, sinks, cos, sin, with_ctx):
    b, t, _ = qkv.shape
    q, k, v = jnp.split(qkv, [C_HEADS * HEAD_DIM, (C_HEADS + C_KV_HEADS) * HEAD_DIM], axis=-1)
    q = rope_latent(q.reshape(b, t, C_HEADS, HEAD_DIM), n_ctx, cos, sin)
    k = rope_latent(k.reshape(b, t, C_KV_HEADS, HEAD_DIM), n_ctx, cos, sin)
    v = v.reshape(b, t, C_KV_HEADS, HEAD_DIM)
    qg, kh, vh = to_groups(q, C_KV_HEADS), to_heads(k), to_heads(v)
    g = C_HEADS // C_KV_HEADS
    scale = HEAD_DIM ** -0.5
    sink = sinks.astype(jnp.float32).reshape(C_KV_HEADS, g)
    k_ctx, v_ctx = kh[:, :, :n_ctx], vh[:, :, :n_ctx]
    q_lat, k_lat, v_lat = qg[:, :, :, n_ctx:], kh[:, :, n_ctx:], vh[:, :, n_ctx:]
    s_len = t - n_ctx
    w = C_WINDOW
    nb = s_len // w
    qb = q_lat.reshape(b, C_KV_HEADS, g, nb, w, HEAD_DIM)

    def band(u):
        up = jnp.pad(u, ((0, 0), (0, 0), (w, w), (0, 0))).reshape(b, C_KV_HEADS, nb + 2, w, u.shape[-1])
        return jnp.concatenate([up[:, :, :-2], up[:, :, 1:-1], up[:, :, 2:]], axis=-2)

    kb, vb = band(k_lat), band(v_lat)
    blk = jnp.arange(nb)[:, None] * w
    q_pos = blk + jnp.arange(w)[None, :]
    k_pos = blk - w + jnp.arange(3 * w)[None, :]
    rel = k_pos[:, None, :] - q_pos[:, :, None]
    valid = (jnp.abs(rel) <= C_WINDOW) & (k_pos[:, None, :] >= 0) & (k_pos[:, None, :] < s_len)
    s_loc = jnp.einsum("bhgnqd,bhnkd->bhgnqk", qb, kb, preferred_element_type=jnp.float32) * scale
    s_loc = jnp.where(valid, s_loc, NEG_BIG)
    s_ctx = jnp.einsum("bhgnqd,bhkd->bhgnqk", qb, k_ctx, preferred_element_type=jnp.float32) * scale
    p_loc, p_ctx = sink_softmax([s_loc, s_ctx], sink[None, :, :, None, None, None])
    o = (jnp.einsum("bhgnqk,bhnkd->bhgnqd", p_loc.astype(vb.dtype), vb)
         + jnp.einsum("bhgnqk,bhkd->bhgnqd", p_ctx.astype(v_ctx.dtype), v_ctx))
    out_lat = merge_heads(o.reshape(b, C_KV_HEADS, g, s_len, HEAD_DIM))
    out_ctx = None
    if with_ctx:
        s_cc = jnp.einsum("bhgqd,bhkd->bhgqk", qg[:, :, :, :n_ctx], k_ctx,
                          preferred_element_type=jnp.float32) * scale
        (p_cc,) = sink_softmax([s_cc], sink[None, :, :, None, None])
        out_ctx = merge_heads(jnp.einsum("bhgqk,bhkd->bhgqd", p_cc.astype(v_ctx.dtype), v_ctx))
    return out_ctx, out_lat


def swiglu(t, w1, w3, w2):
    return (jax.nn.silu(t @ w1) * (t @ w3)) @ w2


def moe_swiglu(t, w_router, w1, w3, w2):
    shape = t.shape
    tok = t.reshape(-1, shape[-1])
    logits = jnp.dot(tok, w_router, preferred_element_type=jnp.float32)
    top_logit, top_idx = lax.top_k(logits, TOP_K)
    gates = jax.nn.softmax(top_logit, axis=-1)
    combine = jnp.sum(jax.nn.one_hot(top_idx, N_EXPERTS, dtype=jnp.float32) * gates[..., None],
                      axis=1).astype(tok.dtype)
    out = jnp.zeros_like(tok)
    for e in range(N_EXPERTS):
        out = out + combine[:, e:e + 1] * swiglu(tok, w1[e], w3[e], w2[e])
    return out.reshape(shape)


def channel_mixer(t, layer, w1_dense, w3_dense, w2_dense, w_router, w1_moe, w3_moe, w2_moe):
    j = layer // 2
    if layer % 2 == 0:
        return swiglu(t, w1_dense[j], w3_dense[j], w2_dense[j])
    return moe_swiglu(t, w_router[j], w1_moe[j], w3_moe[j], w2_moe[j])


def setup_inputs(seed: int = 0) -> dict:
    key = jax.random.key(seed)
    ks = jax.random.split(key, 25)
    f32 = jnp.float32

    def nrm(k, shape, scale=1.0):
        return jax.random.normal(k, shape, f32) * scale

    def gain(k, shape):
        return 1.0 + 0.1 * jax.random.normal(k, shape, f32)

    return {
        "x": nrm(ks[0], (BATCH, SEQ, D_MODEL)),
        "c": nrm(ks[1], (BATCH, D_MODEL)),
        "ctx": nrm(ks[2], (BATCH, CTX_LEN, D_MODEL)),
        "c_ctx": nrm(ks[3], (D_MODEL,)),
        "w_mod": nrm(ks[4], (DEPTH, D_MODEL, 6 * D_MODEL), 0.5 * D_MODEL ** -0.5),
        "b_mod": nrm(ks[5], (DEPTH, 6 * D_MODEL), 0.01),
        "norm_mix": gain(ks[6], (DEPTH, D_MODEL)),
        "norm_ffn": gain(ks[7], (DEPTH, D_MODEL)),
        "w_in": nrm(ks[8], (DEPTH, D_MODEL, IN_COLS), D_MODEL ** -0.5),
        "a_q_norm": gain(ks[9], (DEPTH, HEAD_DIM)),
        "a_k_norm": gain(ks[10], (DEPTH, HEAD_DIM)),
        "b_q_norm": gain(ks[11], (DEPTH, B_Q_RANK)),
        "b_kv_norm": gain(ks[12], (DEPTH, B_KV_RANK)),
        "w_uq": nrm(ks[13], (DEPTH, B_Q_RANK, B_HEADS * (B_NOPE + B_ROPE)), B_Q_RANK ** -0.5),
        "w_ukv": nrm(ks[14], (DEPTH, B_KV_RANK, B_HEADS * (B_NOPE + B_V)), B_KV_RANK ** -0.5),
        "c_sinks": nrm(ks[15], (DEPTH, C_HEADS)),
        "w_out": nrm(ks[16], (DEPTH, MIX_WIDTH, D_MODEL), MIX_WIDTH ** -0.5),
        "w1_dense": nrm(ks[17], (N_DENSE, D_MODEL, D_FF_DENSE), D_MODEL ** -0.5),
        "w3_dense": nrm(ks[18], (N_DENSE, D_MODEL, D_FF_DENSE), D_MODEL ** -0.5),
        "w2_dense": nrm(ks[19], (N_DENSE, D_FF_DENSE, D_MODEL), D_FF_DENSE ** -0.5),
        "w_router": nrm(ks[20], (N_MOE, D_MODEL, N_EXPERTS), D_MODEL ** -0.5),
        "w1_moe": nrm(ks[21], (N_MOE, N_EXPERTS, D_MODEL, D_FF_EXPERT), D_MODEL ** -0.5),
        "w3_moe": nrm(ks[22], (N_MOE, N_EXPERTS, D_MODEL, D_FF_EXPERT), D_MODEL ** -0.5),
        "w2_moe": nrm(ks[23], (N_MOE, N_EXPERTS, D_FF_EXPERT, D_MODEL), D_FF_EXPERT ** -0.5),
        "final_norm": gain(ks[24], (D_MODEL,)),
    }


def reference(x, c, ctx, c_ctx, w_mod, b_mod, norm_mix, norm_ffn, w_in, a_q_norm, a_k_norm,
              b_q_norm, b_kv_norm, w_uq, w_ukv, c_sinks, w_out, w1_dense, w3_dense, w2_dense,
              w_router, w1_moe, w3_moe, w2_moe, final_norm):
    n_ctx = ctx.shape[1]
    n_lat = x.shape[1]
    cos_h, sin_h = axial_rope_tables(n_lat, HEAD_DIM, x.dtype)
    cos_r, sin_r = axial_rope_tables(n_lat, B_ROPE, x.dtype)
    xc = ctx
    for i in range(DEPTH):
        last = i == DEPTH - 1
        mod_lat = jnp.split(jax.nn.silu(c) @ w_mod[i] + b_mod[i], 6, axis=-1)
        sh_a, sc_a, gt_a, sh_f, sc_f, gt_f = [m[:, None, :] for m in mod_lat]
        csh_a, csc_a, cgt_a, csh_f, csc_f, cgt_f = jnp.split(jax.nn.silu(c_ctx) @ w_mod[i] + b_mod[i], 6, axis=-1)

        h = jnp.concatenate([rmsnorm(xc, norm_mix[i]) * (1 + csc_a) + csh_a,
                             rmsnorm(x, norm_mix[i]) * (1 + sc_a) + sh_a], axis=1)
        proj = h @ w_in[i]
        p_a, p_b, p_c = jnp.split(proj, [A_COLS, A_COLS + B_COLS], axis=-1)
        a_ctx, a_lat = global_gqa(p_a, n_ctx, a_q_norm[i], a_k_norm[i], cos_h, sin_h, not last)
        b_ctx, b_lat = latent_attention(p_b, n_ctx, b_q_norm[i], b_kv_norm[i], w_uq[i], w_ukv[i],
                                        cos_r, sin_r, not last)
        c_ctx_out, c_lat = window_sink_gqa(p_c, n_ctx, c_sinks[i], cos_h, sin_h, not last)
        x = x + gt_a * (jnp.concatenate([a_lat, b_lat, c_lat], axis=-1) @ w_out[i])
        if not last:
            xc = xc + cgt_a * (jnp.concatenate([a_ctx, b_ctx, c_ctx_out], axis=-1) @ w_out[i])

        hf_lat = rmsnorm(x, norm_ffn[i]) * (1 + sc_f) + sh_f
        if last:
            x = x + gt_f * channel_mixer(hf_lat, i, w1_dense, w3_dense, w2_dense,
                                         w_router, w1_moe, w3_moe, w2_moe)
        else:
            hf = jnp.concatenate([rmsnorm(xc, norm_ffn[i]) * (1 + csc_f) + csh_f, hf_lat], axis=1)
            y = channel_mixer(hf, i, w1_dense, w3_dense, w2_dense, w_router, w1_moe, w3_moe, w2_moe)
            xc = xc + cgt_f * y[:, :n_ctx]
            x = x + gt_f * y[:, n_ctx:]
    return rmsnorm(x, final_norm)
```

```python
import functools

import jax
import jax.numpy as jnp
from jax import lax
from jax.experimental import pallas as pl
from jax.experimental.pallas import tpu as pltpu

F32 = jnp.float32
BF16 = jnp.bfloat16

LANES = 128
HEAD_DIM = 64
GRID_W = 64
ROPE_THETA = 10000.0
EPS = 1e-6
NEG_BIG = -1e30
A_HEADS, A_KV = 6, 2
B_HEADS, B_Q_RANK, B_KV_RANK, B_NOPE, B_ROPE, B_V = 4, 256, 128, 64, 32, 64
C_HEADS, C_KV, C_WINDOW = 6, 2, 128
N_EXPERTS = 8
A_COLS = (A_HEADS + 2 * A_KV) * HEAD_DIM
B_COLS = B_Q_RANK + B_KV_RANK + B_ROPE
VMEM_LIMIT = 56 * 1024 * 1024

HEAD_PERM = tuple((j % 2) * 3 + j // 2 for j in range(6))


def _cparams(sem):
    return pltpu.CompilerParams(dimension_semantics=sem, vmem_limit_bytes=VMEM_LIMIT)


def _rms(x):
    return x * lax.rsqrt(jnp.mean(x * x, axis=-1, keepdims=True) + EPS)


def _dot(a, b):
    return jnp.dot(a, b, preferred_element_type=F32)


def _dot_nt(a, b):
    return lax.dot_general(a, b, (((1,), (1,)), ((), ())), preferred_element_type=F32)


def _split(a):
    hi = a.astype(BF16)
    return hi, (a - hi.astype(F32)).astype(BF16)


def _dot3(a, b):
    ah, al = _split(a)
    bh, bl = _split(b)
    return _dot(ah, bh) + (_dot(ah, bl) + _dot(al, bh))


def _silu(a):
    return a * jax.nn.sigmoid(a)


def _lane(shape):
    return lax.broadcasted_iota(jnp.int32, shape, len(shape) - 1)


def _mod_kernel(c_ref, w_ref, b_ref, o_ref):
    o_ref[...] = _dot3(_silu(c_ref[...]), w_ref[...]) + b_ref[...]


def _modulation(cc, w_mod, b_mod):
    depth, d, n = w_mod.shape
    rows = cc.shape[0]
    tn = 1536
    return pl.pallas_call(
        _mod_kernel,
        grid=(depth, n // tn),
        in_specs=[pl.BlockSpec((rows, d), lambda i, j: (0, 0)),
                  pl.BlockSpec((None, d, tn), lambda i, j: (i, 0, j)),
                  pl.BlockSpec((None, 1, tn), lambda i, j: (i, 0, j))],
        out_specs=pl.BlockSpec((None, rows, tn), lambda i, j: (i, 0, j)),
        out_shape=jax.ShapeDtypeStruct((depth, rows, n), F32),
        compiler_params=_cparams(("arbitrary", "arbitrary")),
        name="modulation",
    )(cc, w_mod, b_mod.reshape(depth, 1, n))


def _rope(t, cos, sin_signed, half):
    first = (_lane(t.shape) & (2 * half - 1)) < half
    rot = jnp.where(first, pltpu.roll(t, LANES - half, 1), pltpu.roll(t, half, 1))
    return t * cos + rot * sin_signed


def _store_pair(ref, p, t):
    left = _lane(t.shape) < HEAD_DIM
    ref[0, 2 * p] = jnp.where(left, t, 0.0).astype(BF16)
    ref[0, 2 * p + 1] = jnp.where(left, 0.0, t).astype(BF16)


def _proj_kernel(x_ref, mod_ref, g_ref, win_ref, m2_ref, gaq_ref, gak_ref, gbq_ref, gbkv_ref,
                 wuq_ref, wukv_ref, cosa_ref, sina_ref, cosb_ref, sinb_ref,
                 qa_ref, ka_ref, va_ref, qb_ref, kb_ref, vb_ref, qc_ref, kc_ref, vc_ref,
                 *, use_rope):
    mod = mod_ref[0]
    h = _rms(x_ref[0]) * g_ref[...]
    h = h * (1.0 + mod[1:2]) + mod[0:1]
    proj = _dot(h.astype(BF16), win_ref[...])

    def chunk(i):
        return proj[:, i * LANES:(i + 1) * LANES]

    def head_norm(t, gain):
        ms = _dot((t * t).astype(BF16), m2_ref[...])
        return t * lax.rsqrt(ms + EPS) * gain

    def rope_a(t):
        return _rope(t, cosa_ref[...], sina_ref[...], HEAD_DIM // 4) if use_rope else t

    def rope_b(t):
        return _rope(t, cosb_ref[...], sinb_ref[...], B_ROPE // 4) if use_rope else t

    scale = HEAD_DIM ** -0.5
    for p in range(3):
        _store_pair(qa_ref, p, rope_a(head_norm(chunk(p), gaq_ref[...])) * scale)
    ka_ref[0, 0] = rope_a(head_norm(chunk(3), gak_ref[...])).astype(BF16)
    va_ref[0, 0] = chunk(4).astype(BF16)
    cq = _rms(proj[:, 5 * LANES:7 * LANES]) * gbq_ref[...]
    qb = _dot(cq.astype(BF16), wuq_ref[...])
    ckv = _rms(chunk(7)) * gbkv_ref[...]
    kvb = _dot(ckv.astype(BF16), wukv_ref[...])
    k_rope = rope_b(chunk(8))
    scale_b = (B_NOPE + B_ROPE) ** -0.5
    for hd in range(B_HEADS):
        qb_ref[0, hd] = (rope_b(qb[:, hd * LANES:(hd + 1) * LANES]) * scale_b).astype(BF16)
        kb_ref[0, hd] = (kvb[:, hd * LANES:(hd + 1) * LANES] + k_rope).astype(BF16)
    for p in range(2):
        vb_ref[0, p] = kvb[:, (B_HEADS + p) * LANES:(B_HEADS + p + 1) * LANES].astype(BF16)
    for p in range(3):
        _store_pair(qc_ref, p, rope_a(chunk(9 + p)) * scale)
    kc_ref[0, 0] = rope_a(chunk(12)).astype(BF16)
    vc_ref[0, 0] = chunk(13).astype(BF16)


def _in_projection(x, mod, per_batch_mod, g, lw, tables, use_rope, tm):
    b, t, d = x.shape
    ncol = lw["w_in"].shape[1]
    heads = (6, 1, 1, 4, 4, 2, 6, 1, 1)

    def const(shape):
        return pl.BlockSpec(shape, lambda i, j: (0,) * len(shape))

    mod_map = (lambda i, j: (i, 0, 0)) if per_batch_mod else (lambda i, j: (0, 0, 0))
    tab = pl.BlockSpec((tm, LANES), lambda i, j: (j, 0))
    return pl.pallas_call(
        functools.partial(_proj_kernel, use_rope=use_rope),
        grid=(b, t // tm),
        in_specs=[pl.BlockSpec((1, tm, d), lambda i, j: (i, j, 0)),
                  pl.BlockSpec((1, 6, d), mod_map),
                  const((1, d)), const((d, ncol)), const((LANES, LANES)),
                  const((1, LANES)), const((1, LANES)), const((1, B_Q_RANK)), const((1, B_KV_RANK)),
                  const((B_Q_RANK, 4 * LANES)), const((B_KV_RANK, 6 * LANES)),
                  tab, tab, tab, tab],
        out_specs=[pl.BlockSpec((1, n, tm, LANES), lambda i, j: (i, 0, j, 0)) for n in heads],
        out_shape=[jax.ShapeDtypeStruct((b, n, t, LANES), BF16) for n in heads],
        compiler_params=_cparams(("parallel", "parallel")),
        name="in_projection",
    )(x, mod, g, lw["w_in"], lw["m2"], lw["a_q_gain"], lw["a_k_gain"], lw["b_q_gain"],
      lw["b_kv_gain"], lw["w_uq"], lw["w_ukv"], *tables)


def _flash_kernel(sink_ref, q_ref, k_ref, v_ref, o_ref, *, kidx, vidx, tk, use_sink):
    nh, tq = q_ref.shape[1], q_ref.shape[2]
    t_keys = k_ref.shape[2]
    nfull, tail = t_keys // tk, t_keys % tk
    left = _lane((tq, LANES)) < HEAD_DIM
    held = None
    for j in range(nh):
        q = q_ref[0, j]

        def step(start, size, carry, j=j, q=q):
            m, l, acc = carry
            k = k_ref[0, kidx[j], pl.ds(start, size), :]
            v = v_ref[0, vidx[j], pl.ds(start, size), :]
            s = _dot_nt(q, k)
            m_new = jnp.maximum(m, jnp.max(s, axis=-1, keepdims=True))
            alpha = jnp.exp(m - m_new)
            p = jnp.exp(s - m_new)
            l = alpha * l + jnp.sum(p, axis=-1, keepdims=True)
            acc = alpha * acc + _dot(p.astype(BF16), v)
            return m_new, l, acc

        if use_sink:
            carry = (jnp.full((tq, 1), sink_ref[j], F32), jnp.ones((tq, 1), F32))
        else:
            carry = (jnp.full((tq, 1), NEG_BIG, F32), jnp.zeros((tq, 1), F32))
        carry = carry + (jnp.zeros((tq, LANES), F32),)
        if nfull:
            carry = lax.fori_loop(
                0, nfull, lambda c, cr: step(pl.multiple_of(c * tk, tk), tk, cr), carry)
        if tail:
            carry = step(nfull * tk, tail, carry)
        out = carry[2] / carry[1]
        if j % 2 == 0:
            held = out
        else:
            o_ref[0, j // 2] = jnp.where(left, held, out).astype(BF16)


def _flash(q, k, v, sinks, kidx, vidx, use_sink, tq):
    b, nh, t, _ = q.shape
    nk, tkeys = k.shape[1], k.shape[2]
    nv = v.shape[1]
    return pl.pallas_call(
        functools.partial(_flash_kernel, kidx=kidx, vidx=vidx, tk=512, use_sink=use_sink),
        grid=(b, t // tq),
        in_specs=[pl.BlockSpec(memory_space=pltpu.SMEM),
                  pl.BlockSpec((1, nh, tq, LANES), lambda i, j: (i, 0, j, 0)),
                  pl.BlockSpec((1, nk, tkeys, LANES), lambda i, j: (i, 0, 0, 0)),
                  pl.BlockSpec((1, nv, tkeys, LANES), lambda i, j: (i, 0, 0, 0))],
        out_specs=pl.BlockSpec((1, nh // 2, tq, LANES), lambda i, j: (i, 0, j, 0)),
        out_shape=jax.ShapeDtypeStruct((b, nh // 2, t, LANES), BF16),
        compiler_params=_cparams(("parallel", "parallel")),
        name="flash_attention",
    )(sinks, q, k, v)


def _window_kernel(sink_ref, q_ref, kp_ref, kc_ref, kn_ref, kx_ref, vp_ref, vc_ref, vn_ref, vx_ref,
                   o_ref):
    nh, w = q_ref.shape[1], q_ref.shape[2]
    n_ctx = kx_ref.shape[2]
    n, nb = pl.program_id(1), pl.num_programs(1)
    q = q_ref[0].reshape(nh * w, LANES)
    k = jnp.concatenate([kp_ref[0, 0], kc_ref[0, 0], kn_ref[0, 0], kx_ref[0, 0]], axis=0)
    v = jnp.concatenate([vp_ref[0, 0], vc_ref[0, 0], vn_ref[0, 0], vx_ref[0, 0]], axis=0)
    s = _dot_nt(q, k)
    shape = (nh * w, 3 * w + n_ctx)
    qq = lax.broadcasted_iota(jnp.int32, shape, 0) & (w - 1)
    col = lax.broadcasted_iota(jnp.int32, shape, 1)
    never = 4 * w
    off_prev = jnp.where(n >= 1, 0, never)
    off_next = jnp.where(n <= nb - 2, 0, never)
    valid = ((col < w) & (col >= qq + off_prev)) | ((col >= w) & (col < 2 * w)) \
        | ((col >= 2 * w) & (col < 3 * w) & (col - 2 * w + off_next <= qq)) | (col >= 3 * w)
    s = jnp.where(valid, s, NEG_BIG)
    sink = jnp.concatenate([jnp.full((w, 1), sink_ref[j], F32) for j in range(nh)], axis=0)
    m = jnp.maximum(sink, jnp.max(s, axis=-1, keepdims=True))
    e = jnp.exp(s - m)
    denom = jnp.exp(sink - m) + jnp.sum(e, axis=-1, keepdims=True)
    o = _dot(e.astype(BF16), v) / denom
    left = _lane((w, LANES)) < HEAD_DIM
    for p in range(nh // 2):
        o_ref[0, p] = jnp.where(left, o[2 * p * w:(2 * p + 1) * w],
                                o[(2 * p + 1) * w:(2 * p + 2) * w]).astype(BF16)


def _window_attention(q, k_all, v_all, sinks, n_lat):
    b, nh, _, _ = q.shape
    w = C_WINDOW
    nb = n_lat // w
    n_ctx = k_all.shape[2] - n_lat
    assert n_lat % n_ctx == 0 and w & (w - 1) == 0
    prev = pl.BlockSpec((1, 1, w, LANES), lambda i, j: (i, 0, jnp.maximum(j - 1, 0), 0))
    cur = pl.BlockSpec((1, 1, w, LANES), lambda i, j: (i, 0, j, 0))
    nxt = pl.BlockSpec((1, 1, w, LANES), lambda i, j: (i, 0, jnp.minimum(j + 1, nb - 1), 0))
    ctx = pl.BlockSpec((1, 1, n_ctx, LANES), lambda i, j: (i, 0, n_lat // n_ctx, 0))
    return pl.pallas_call(
        _window_kernel,
        grid=(b, nb),
        in_specs=[pl.BlockSpec(memory_space=pltpu.SMEM),
                  pl.BlockSpec((1, nh, w, LANES), lambda i, j: (i, 0, j, 0)),
                  prev, cur, nxt, ctx, prev, cur, nxt, ctx],
        out_specs=pl.BlockSpec((1, nh // 2, w, LANES), lambda i, j: (i, 0, j, 0)),
        out_shape=jax.ShapeDtypeStruct((b, nh // 2, n_lat, LANES), BF16),
        compiler_params=_cparams(("parallel", "parallel")),
        name="window_attention",
    )(sinks, q, k_all, k_all, k_all, k_all, v_all, v_all, v_all, v_all)


def _mixed_residual(x_ref, oa_ref, ob_ref, oc_ref, mod, wout_ref):
    att = jnp.concatenate([oa_ref[0, 0], oa_ref[0, 1], oa_ref[0, 2], ob_ref[0, 0], ob_ref[0, 1],
                           oc_ref[0, 0], oc_ref[0, 1], oc_ref[0, 2]], axis=-1)
    return x_ref[0] + mod[2:3] * _dot(att, wout_ref[...])


def _ffn_dense_kernel(x_ref, oa_ref, ob_ref, oc_ref, mod_ref, wout_ref, g_ref, w1_ref, w3_ref, w2_ref,
                      o_ref, *, n_chunks):
    mod = mod_ref[0]
    x1 = _mixed_residual(x_ref, oa_ref, ob_ref, oc_ref, mod, wout_ref)
    hf = ((_rms(x1) * g_ref[...]) * (1.0 + mod[4:5]) + mod[3:4]).astype(BF16)
    fc = w1_ref.shape[1] // n_chunks
    y = jnp.zeros(x1.shape, F32)
    for c in range(n_chunks):
        a = _dot(hf, w1_ref[:, c * fc:(c + 1) * fc])
        gate = _dot(hf, w3_ref[:, c * fc:(c + 1) * fc])
        y = y + _dot((_silu(a) * gate).astype(BF16), w2_ref[c * fc:(c + 1) * fc, :])
    o_ref[0] = x1 + mod[5:6] * y


def _resident(shape):
    return pl.BlockSpec(shape, lambda i, j: (0,) * len(shape), pipeline_mode=pl.Buffered(1))


def _ffn_dense(x, oa, ob, oc, mod, per_batch_mod, lw, tm):
    b, t, d = x.shape
    f = lw["w1"].shape[1]
    mod_map = (lambda i, j: (i, 0, 0)) if per_batch_mod else (lambda i, j: (0, 0, 0))

    def att(n):
        return pl.BlockSpec((1, n, tm, LANES), lambda i, j: (i, 0, j, 0))

    return pl.pallas_call(
        functools.partial(_ffn_dense_kernel, n_chunks=2),
        grid=(b, t // tm),
        in_specs=[pl.BlockSpec((1, tm, d), lambda i, j: (i, j, 0)), att(3), att(2), att(3),
                  pl.BlockSpec((1, 6, d), mod_map),
                  _resident((d, d)), _resident((1, d)),
                  _resident((d, f)), _resident((d, f)), _resident((f, d))],
        out_specs=pl.BlockSpec((1, tm, d), lambda i, j: (i, j, 0)),
        out_shape=jax.ShapeDtypeStruct((b, t, d), F32),
        compiler_params=_cparams(("parallel", "parallel")),
        name="ffn_dense",
    )(x, oa, ob, oc, mod, lw["w_out"], lw["g_ffn"], lw["w1"], lw["w3"], lw["w2"])


def _route(hf, wr_ref):
    logits = _dot3(hf, wr_ref[...])
    lane = _lane(logits.shape)
    logits = jnp.where(lane < N_EXPERTS, logits, NEG_BIG)
    m1 = jnp.max(logits, axis=-1, keepdims=True)
    i1 = jnp.min(jnp.where(logits == m1, lane, LANES), axis=-1, keepdims=True)
    rest = jnp.where(lane == i1, NEG_BIG, logits)
    m2 = jnp.max(rest, axis=-1, keepdims=True)
    i2 = jnp.min(jnp.where(rest == m2, lane, LANES), axis=-1, keepdims=True)
    e2 = jnp.exp(m2 - m1)
    g1 = 1.0 / (1.0 + e2)
    return jnp.where(lane == i1, g1, 0.0) + jnp.where(lane == i2, e2 * g1, 0.0)


def _ffn_moe_kernel(x_ref, oa_ref, ob_ref, oc_ref, mod_ref, wout_ref, g_ref, wr_ref, gfin_ref,
                    w1_ref, w3_ref, w2_ref, o_ref, x1_scr, hf_scr, comb_scr, acc_scr):
    e, f = pl.program_id(2), pl.program_id(3)
    mod = mod_ref[0]

    @pl.when((e == 0) & (f == 0))
    def _():
        x1 = _mixed_residual(x_ref, oa_ref, ob_ref, oc_ref, mod, wout_ref)
        hf = (_rms(x1) * g_ref[...]) * (1.0 + mod[4:5]) + mod[3:4]
        x1_scr[...] = x1
        hf_scr[...] = hf.astype(BF16)
        comb_scr[...] = _route(hf, wr_ref)
        acc_scr[...] = jnp.zeros(acc_scr.shape, F32)

    hf = hf_scr[...]
    a = _dot(hf, w1_ref[...])
    gate = _dot(hf, w3_ref[...])
    y = _dot((_silu(a) * gate).astype(BF16), w2_ref[...])
    comb = comb_scr[...]
    ge = jnp.sum(jnp.where(_lane(comb.shape) == e, comb, 0.0), axis=-1, keepdims=True)
    acc_scr[...] += ge * y

    @pl.when((e == pl.num_programs(2) - 1) & (f == pl.num_programs(3) - 1))
    def _():
        o_ref[0] = _rms(x1_scr[...] + mod[5:6] * acc_scr[...]) * gfin_ref[...]


def _ffn_moe(x, oa, ob, oc, mod, lw, gfin, tm, tf):
    b, t, d = x.shape
    ne, _, f = lw["w1"].shape

    def att(n):
        return pl.BlockSpec((1, n, tm, LANES), lambda i, j, e, k: (i, 0, j, 0))

    def const(shape):
        return pl.BlockSpec(shape, lambda i, j, e, k: (0,) * len(shape))

    return pl.pallas_call(
        _ffn_moe_kernel,
        grid=(b, t // tm, ne, f // tf),
        in_specs=[pl.BlockSpec((1, tm, d), lambda i, j, e, k: (i, j, 0)), att(3), att(2), att(3),
                  pl.BlockSpec((1, 6, d), lambda i, j, e, k: (i, 0, 0)),
                  const((d, d)), const((1, d)), const((d, LANES)), const((1, d)),
                  pl.BlockSpec((None, d, tf), lambda i, j, e, k: (e, 0, k)),
                  pl.BlockSpec((None, d, tf), lambda i, j, e, k: (e, 0, k)),
                  pl.BlockSpec((None, tf, d), lambda i, j, e, k: (e, k, 0))],
        out_specs=pl.BlockSpec((1, tm, d), lambda i, j, e, k: (i, j, 0)),
        out_shape=jax.ShapeDtypeStruct((b, t, d), F32),
        scratch_shapes=[pltpu.VMEM((tm, d), F32), pltpu.VMEM((tm, d), BF16),
                        pltpu.VMEM((tm, LANES), F32), pltpu.VMEM((tm, d), F32)],
        compiler_params=_cparams(("parallel", "parallel", "arbitrary", "arbitrary")),
        name="ffn_moe",
    )(x, oa, ob, oc, mod, lw["w_out"], lw["g_ffn"], lw["w_router"], gfin,
      lw["w1"], lw["w3"], lw["w2"])


def _rope_tables(n_lat):
    rows = n_lat // GRID_W
    row = jnp.repeat(jnp.arange(rows, dtype=F32), GRID_W)
    col = jnp.tile(jnp.arange(GRID_W, dtype=F32), rows)

    def table(rot_dim):
        nf = rot_dim // 4
        inv = ROPE_THETA ** (-jnp.arange(nf, dtype=F32) / nf)
        ang_r, ang_c = row[:, None] * inv, col[:, None] * inv
        ang = jnp.concatenate([ang_r, ang_r, ang_c, ang_c], axis=-1)
        sign = jnp.tile(jnp.concatenate([-jnp.ones(nf, F32), jnp.ones(nf, F32)]), 2)
        return jnp.cos(ang), jnp.sin(ang) * sign

    cos_h, sin_h = table(HEAD_DIM)
    cos_r, sin_r = table(B_ROPE)
    one = jnp.ones((n_lat, B_NOPE), F32)
    pad = LANES - B_NOPE - B_ROPE
    cos_b = jnp.concatenate([one, cos_r, jnp.ones((n_lat, pad), F32)], axis=-1)
    sin_b = jnp.concatenate([0 * one, sin_r, jnp.zeros((n_lat, pad), F32)], axis=-1)
    return (jnp.tile(cos_h, (1, 2)), jnp.tile(sin_h, (1, 2)), cos_b, sin_b)


def _permute_heads(w, axis):
    shape = w.shape
    w = w.reshape(shape[:axis] + (6, HEAD_DIM) + shape[axis + 1:])
    w = jnp.take(w, jnp.array(HEAD_PERM), axis=axis)
    return w.reshape(shape)


def _layer_weights(i, w_in, norm_ffn, a_q_norm, a_k_norm, b_q_norm, b_kv_norm, w_uq, w_ukv, w_out):
    d = w_in.shape[1]
    wi = w_in[i]
    qk = A_HEADS * HEAD_DIM
    kv = A_KV * HEAD_DIM
    b0 = A_COLS
    c0 = A_COLS + B_COLS
    k_rope = jnp.pad(wi[:, b0 + B_Q_RANK + B_KV_RANK:c0], ((0, 0), (B_NOPE, LANES - B_NOPE - B_ROPE)))
    w_in_p = jnp.concatenate([
        _permute_heads(wi[:, :qk], 1), wi[:, qk:A_COLS],
        wi[:, b0:b0 + B_Q_RANK + B_KV_RANK], k_rope,
        _permute_heads(wi[:, c0:c0 + qk], 1), wi[:, c0 + qk:]], axis=1).astype(BF16)
    w_uq_p = jnp.pad(w_uq[i].reshape(B_Q_RANK, B_HEADS, B_NOPE + B_ROPE),
                     ((0, 0), (0, 0), (0, LANES - B_NOPE - B_ROPE))).reshape(B_Q_RANK, B_HEADS * LANES)
    ukv = w_ukv[i].reshape(B_KV_RANK, B_HEADS, B_NOPE + B_V)
    k_up = jnp.pad(ukv[:, :, :B_NOPE], ((0, 0), (0, 0), (0, LANES - B_NOPE))).reshape(B_KV_RANK, -1)
    v_up = ukv[:, :, B_NOPE:].reshape(B_KV_RANK, -1)
    wo = w_out[i]
    w_out_p = jnp.concatenate([_permute_heads(wo[:qk], 0), wo[qk:qk + B_HEADS * B_V],
                               _permute_heads(wo[qk + B_HEADS * B_V:], 0)], axis=0).astype(BF16)
    blk = (jnp.arange(LANES)[:, None] // HEAD_DIM) == (jnp.arange(LANES)[None, :] // HEAD_DIM)
    del kv, d
    return {
        "w_in": w_in_p,
        "m2": (blk.astype(F32) / HEAD_DIM).astype(BF16),
        "a_q_gain": jnp.tile(a_q_norm[i], 2)[None], "a_k_gain": jnp.tile(a_k_norm[i], 2)[None],
        "b_q_gain": b_q_norm[i][None], "b_kv_gain": b_kv_norm[i][None],
        "w_uq": w_uq_p.astype(BF16),
        "w_ukv": jnp.concatenate([k_up, v_up], axis=1).astype(BF16),
        "w_out": w_out_p,
        "g_ffn": norm_ffn[i][None],
    }


def kernel(x, c, ctx, c_ctx, w_mod, b_mod, norm_mix, norm_ffn, w_in, a_q_norm, a_k_norm, b_q_norm,
           b_kv_norm, w_uq, w_ukv, c_sinks, w_out, w1_dense, w3_dense, w2_dense, w_router, w1_moe,
           w3_moe, w2_moe, final_norm):
    bsz, n_lat, d = x.shape
    n_ctx = ctx.shape[1]
    depth = w_mod.shape[0]
    mod_rows = 16
    cc = jnp.concatenate([c, c_ctx[None], jnp.zeros((mod_rows - bsz - 1, d), F32)], axis=0)
    mods = _modulation(cc, w_mod, b_mod)
    tables = _rope_tables(n_lat)
    no_sink = jnp.zeros((A_HEADS,), F32)
    xc = ctx
    for i in range(depth):
        last = i == depth - 1
        mod_lat = mods[i, :bsz].reshape(bsz, 6, d)
        mod_ctx = mods[i, bsz:bsz + 1].reshape(1, 6, d)
        lw = _layer_weights(i, w_in, norm_ffn, a_q_norm, a_k_norm, b_q_norm, b_kv_norm, w_uq, w_ukv,
                            w_out)
        g_mix = norm_mix[i][None]
        lat = _in_projection(x, mod_lat, True, g_mix, lw, tables, True, 512)
        cx = _in_projection(xc, mod_ctx, False, g_mix, lw, tables, False, n_ctx)
        qa, ka, va, qb, kb, vb, qc, kc, vc = lat
        cqa, cka, cva, cqb, ckb, cvb, cqc, ckc, cvc = cx

        def both(u, w):
            return jnp.concatenate([u, w], axis=2)

        sinks = jnp.take(c_sinks[i], jnp.array(HEAD_PERM))
        oa = _flash(qa, both(ka, cka), both(va, cva), no_sink, (0,) * 6, (0,) * 6, False, 512)
        ob = _flash(qb, both(kb, ckb), both(vb, cvb), no_sink, (0, 1, 2, 3), (0, 0, 1, 1), False, 512)
        oc = _window_attention(qc, both(kc, ckc), both(vc, cvc), sinks, n_lat)
        if not last:
            coa = _flash(cqa, cka, cva, no_sink, (0,) * 6, (0,) * 6, False, n_ctx)
            cob = _flash(cqb, ckb, cvb, no_sink, (0, 1, 2, 3), (0, 0, 1, 1), False, n_ctx)
            coc = _flash(cqc, ckc, cvc, sinks, (0,) * 6, (0,) * 6, True, n_ctx)
        j = i // 2
        if i % 2 == 0:
            lw.update(w1=w1_dense[j].astype(BF16), w3=w3_dense[j].astype(BF16),
                      w2=w2_dense[j].astype(BF16))
            x_new = _ffn_dense(x, oa, ob, oc, mod_lat, True, lw, 512)
            if not last:
                xc = _ffn_dense(xc, coa, cob, coc, mod_ctx, False, lw, n_ctx)
            x = x_new
        else:
            assert last, "the expert channel mixer is fused with the final norm"
            lw.update(w1=w1_moe[j].astype(BF16), w3=w3_moe[j].astype(BF16), w2=w2_moe[j].astype(BF16),
                      w_router=jnp.pad(w_router[j], ((0, 0), (0, LANES - N_EXPERTS))))
            x = _ffn_moe(x, oa, ob, oc, mod_lat, lw, final_norm[None], 512, 896)
    return x
```

```python
import functools

import jax
import jax.numpy as jnp
from jax import lax
from jax.experimental import pallas as pl
from jax.experimental.pallas import tpu as pltpu

F32 = jnp.float32
BF16 = jnp.bfloat16

LANES = 128
HEAD_DIM = 64
GRID_W = 64
ROPE_THETA = 10000.0
EPS = 1e-6
NEG_BIG = -1e30
A_HEADS, A_KV = 6, 2
B_HEADS, B_Q_RANK, B_KV_RANK, B_NOPE, B_ROPE, B_V = 4, 256, 128, 64, 32, 64
C_HEADS, C_KV, C_WINDOW = 6, 2, 128
N_EXPERTS = 8
A_COLS = (A_HEADS + 2 * A_KV) * HEAD_DIM
B_COLS = B_Q_RANK + B_KV_RANK + B_ROPE
VMEM_LIMIT = 56 * 1024 * 1024
LOG2E = 1.4426950408889634
FLASH_TK = 256
MOE_TM = 1024
MOE_TF = 896
ROW_TILE = 1024

HEAD_PERM = tuple((j % 2) * 3 + j // 2 for j in range(6))


def _cparams(sem):
    return pltpu.CompilerParams(dimension_semantics=sem, vmem_limit_bytes=VMEM_LIMIT)


def _rms(x):
    return x * lax.rsqrt(jnp.mean(x * x, axis=-1, keepdims=True) + EPS)


def _dot(a, b):
    return jnp.dot(a, b, preferred_element_type=F32)


def _dot_nt(a, b):
    return lax.dot_general(a, b, (((1,), (1,)), ((), ())), preferred_element_type=F32)


def _split(a):
    hi = a.astype(BF16)
    return hi, (a - hi.astype(F32)).astype(BF16)


def _dot3(a, b):
    ah, al = _split(a)
    bh, bl = _split(b)
    return _dot(ah, bh) + (_dot(ah, bl) + _dot(al, bh))


def _silu(a):
    return a * jax.nn.sigmoid(a)


def _lane(shape):
    return lax.broadcasted_iota(jnp.int32, shape, len(shape) - 1)


def _mod_kernel(c_ref, w_ref, b_ref, o_ref):
    o_ref[...] = _dot3(_silu(c_ref[...]), w_ref[...]) + b_ref[...]


def _modulation(cc, w_mod, b_mod):
    depth, d, n = w_mod.shape
    rows = cc.shape[0]
    tn = 1536
    return pl.pallas_call(
        _mod_kernel,
        grid=(depth, n // tn),
        in_specs=[pl.BlockSpec((rows, d), lambda i, j: (0, 0)),
                  pl.BlockSpec((None, d, tn), lambda i, j: (i, 0, j)),
                  pl.BlockSpec((None, 1, tn), lambda i, j: (i, 0, j))],
        out_specs=pl.BlockSpec((None, rows, tn), lambda i, j: (i, 0, j)),
        out_shape=jax.ShapeDtypeStruct((depth, rows, n), F32),
        compiler_params=_cparams(("arbitrary", "arbitrary")),
        name="modulation",
    )(cc, w_mod, b_mod.reshape(depth, 1, n))


def _rope(t, cos, sin_signed, half):
    first = (_lane(t.shape) & (2 * half - 1)) < half
    rot = jnp.where(first, pltpu.roll(t, LANES - half, 1), pltpu.roll(t, half, 1))
    return t * cos + rot * sin_signed


def _store_pair(ref, p, t):
    left = _lane(t.shape) < HEAD_DIM
    ref[0, 2 * p] = jnp.where(left, t, 0.0).astype(BF16)
    ref[0, 2 * p + 1] = jnp.where(left, 0.0, t).astype(BF16)


def _store_transposed(ref, p, t):
    tt = t.T.astype(BF16)
    for c in range(ref.shape[2]):
        ref[0, p, c] = tt[:, c * FLASH_TK:(c + 1) * FLASH_TK]


def _proj_kernel(x_ref, mod_ref, g_ref, win_ref, m2_ref, gaq_ref, gak_ref, gbq_ref, gbkv_ref,
                 wuq_ref, wukv_ref, cosa_ref, sina_ref, cosb_ref, sinb_ref,
                 qa_ref, ka_ref, va_ref, qb_ref, kb_ref, vb_ref, qc_ref, kc_ref, vc_ref,
                 *, use_rope):
    mod = mod_ref[0]
    h = _rms(x_ref[0]) * g_ref[...]
    h = h * (1.0 + mod[1:2]) + mod[0:1]
    proj = _dot(h.astype(BF16), win_ref[...])

    def chunk(i):
        return proj[:, i * LANES:(i + 1) * LANES]

    def head_norm(t, gain):
        ms = _dot((t * t).astype(BF16), m2_ref[...])
        return t * lax.rsqrt(ms + EPS) * gain

    def rope_a(t):
        return _rope(t, cosa_ref[...], sina_ref[...], HEAD_DIM // 4) if use_rope else t

    def rope_b(t):
        return _rope(t, cosb_ref[...], sinb_ref[...], B_ROPE // 4) if use_rope else t

    scale = HEAD_DIM ** -0.5 * LOG2E
    for p in range(3):
        _store_pair(qa_ref, p, rope_a(head_norm(chunk(p), gaq_ref[...])) * scale)
    ka_ref[0, 0] = rope_a(head_norm(chunk(3), gak_ref[...])).astype(BF16)
    _store_transposed(va_ref, 0, chunk(4))
    cq = _rms(proj[:, 5 * LANES:7 * LANES]) * gbq_ref[...]
    qb = _dot(cq.astype(BF16), wuq_ref[...])
    ckv = _rms(chunk(7)) * gbkv_ref[...]
    kvb = _dot(ckv.astype(BF16), wukv_ref[...])
    k_rope = rope_b(chunk(8))
    scale_b = (B_NOPE + B_ROPE) ** -0.5 * LOG2E
    for hd in range(B_HEADS):
        qb_ref[0, hd] = (rope_b(qb[:, hd * LANES:(hd + 1) * LANES]) * scale_b).astype(BF16)
        kb_ref[0, hd] = (kvb[:, hd * LANES:(hd + 1) * LANES] + k_rope).astype(BF16)
    for p in range(2):
        _store_transposed(vb_ref, p, kvb[:, (B_HEADS + p) * LANES:(B_HEADS + p + 1) * LANES])
    for p in range(3):
        _store_pair(qc_ref, p, rope_a(chunk(9 + p)) * scale)
    kc_ref[0, 0] = rope_a(chunk(12)).astype(BF16)
    vc_ref[0, 0] = chunk(13).astype(BF16)


def _in_projection(x, mod, per_batch_mod, g, lw, tables, use_rope, tm):
    b, t, d = x.shape
    ncol = lw["w_in"].shape[1]
    heads = ((6, "r"), (1, "r"), (1, "t"), (4, "r"), (4, "r"), (2, "t"), (6, "r"), (1, "r"), (1, "r"))

    def row_major(n):
        return pl.BlockSpec((1, n, tm, LANES), lambda i, j: (i, 0, j, 0))

    def key_chunked(n):
        return pl.BlockSpec((1, n, tm // FLASH_TK, LANES, FLASH_TK), lambda i, j: (i, 0, j, 0, 0))

    def const(shape):
        return pl.BlockSpec(shape, lambda i, j: (0,) * len(shape))

    mod_map = (lambda i, j: (i, 0, 0)) if per_batch_mod else (lambda i, j: (0, 0, 0))
    tab = pl.BlockSpec((tm, LANES), lambda i, j: (j, 0))
    return pl.pallas_call(
        functools.partial(_proj_kernel, use_rope=use_rope),
        grid=(b, t // tm),
        in_specs=[pl.BlockSpec((1, tm, d), lambda i, j: (i, j, 0)),
                  pl.BlockSpec((1, 6, d), mod_map),
                  const((1, d)), const((d, ncol)), const((LANES, LANES)),
                  const((1, LANES)), const((1, LANES)), const((1, B_Q_RANK)), const((1, B_KV_RANK)),
                  const((B_Q_RANK, 4 * LANES)), const((B_KV_RANK, 6 * LANES)),
                  tab, tab, tab, tab],
        out_specs=[row_major(n) if kind == "r" else key_chunked(n) for n, kind in heads],
        out_shape=[jax.ShapeDtypeStruct(
            (b, n, t, LANES) if kind == "r" else (b, n, t // FLASH_TK, LANES, FLASH_TK), BF16)
            for n, kind in heads],
        compiler_params=_cparams(("parallel", "parallel")),
        name="in_projection",
    )(x, mod, g, lw["w_in"], lw["m2"], lw["a_q_gain"], lw["a_k_gain"], lw["b_q_gain"],
      lw["b_kv_gain"], lw["w_uq"], lw["w_ukv"], *tables)


def _flash_kernel(sink_ref, q_ref, k_ref, vt_ref, o_ref, *, kidx, vidx, use_sink):
    nh, tq = q_ref.shape[1], q_ref.shape[2]
    n_chunks, tk = vt_ref.shape[2], vt_ref.shape[4]

    def init(j):
        if use_sink:
            m, l = jnp.full((1, tq), sink_ref[j] * LOG2E, F32), jnp.ones((1, tq), F32)
        else:
            m, l = jnp.full((1, tq), NEG_BIG, F32), jnp.zeros((1, tq), F32)
        return m, l, jnp.zeros((LANES, tq), F32)

    def body(c, state):
        start = pl.multiple_of(c * tk, tk)
        new = []
        for j in range(nh):
            m, l, acc = state[j]
            k = k_ref[0, kidx[j], pl.ds(start, tk), :]
            s = _dot_nt(k, q_ref[0, j])
            m_new = jnp.maximum(m, jnp.max(s, axis=0, keepdims=True))
            alpha = jnp.exp2(m - m_new)
            p = jnp.exp2(s - m_new)
            l = alpha * l + jnp.sum(p, axis=0, keepdims=True)
            acc = alpha * acc + _dot(vt_ref[0, vidx[j], c], p.astype(BF16))
            new.append((m_new, l, acc))
        return tuple(new)

    state = lax.fori_loop(0, n_chunks, body, tuple(init(j) for j in range(nh)))
    top = lax.broadcasted_iota(jnp.int32, (LANES, tq), 0) < HEAD_DIM
    for p in range(nh // 2):
        even = state[2 * p][2] / state[2 * p][1]
        odd = state[2 * p + 1][2] / state[2 * p + 1][1]
        o_ref[0, p] = jnp.where(top, even, odd).T.astype(BF16)


def _flash(q, k, vt, sinks, kidx, vidx, use_sink, tq):
    b, nh, t, _ = q.shape
    nk, tkeys = k.shape[1], k.shape[2]
    nv, n_chunks = vt.shape[1], vt.shape[2]
    assert n_chunks * FLASH_TK == tkeys
    return pl.pallas_call(
        functools.partial(_flash_kernel, kidx=kidx, vidx=vidx, use_sink=use_sink),
        grid=(b, t // tq),
        in_specs=[pl.BlockSpec(memory_space=pltpu.SMEM),
                  pl.BlockSpec((1, nh, tq, LANES), lambda i, j: (i, 0, j, 0)),
                  pl.BlockSpec((1, nk, tkeys, LANES), lambda i, j: (i, 0, 0, 0)),
                  pl.BlockSpec((1, nv, n_chunks, LANES, FLASH_TK), lambda i, j: (i, 0, 0, 0, 0))],
        out_specs=pl.BlockSpec((1, nh // 2, tq, LANES), lambda i, j: (i, 0, j, 0)),
        out_shape=jax.ShapeDtypeStruct((b, nh // 2, t, LANES), BF16),
        compiler_params=_cparams(("parallel", "parallel")),
        name="flash_attention",
    )(sinks, q, k, vt)


def _window_kernel(sink_ref, q_ref, kp_ref, kc_ref, kn_ref, kx_ref, vp_ref, vc_ref, vn_ref, vx_ref,
                   o_ref):
    nh, w = q_ref.shape[1], q_ref.shape[2]
    n_ctx = kx_ref.shape[2]
    n, nb = pl.program_id(1), pl.num_programs(1)
    q = q_ref[0].reshape(nh * w, LANES)
    k = jnp.concatenate([kp_ref[0, 0], kc_ref[0, 0], kn_ref[0, 0], kx_ref[0, 0]], axis=0)
    v = jnp.concatenate([vp_ref[0, 0], vc_ref[0, 0], vn_ref[0, 0], vx_ref[0, 0]], axis=0)
    s = _dot_nt(q, k)
    shape = (nh * w, 3 * w + n_ctx)
    qq = lax.broadcasted_iota(jnp.int32, shape, 0) & (w - 1)
    col = lax.broadcasted_iota(jnp.int32, shape, 1)
    never = 4 * w
    off_prev = jnp.where(n >= 1, 0, never)
    off_next = jnp.where(n <= nb - 2, 0, never)
    valid = ((col < w) & (col >= qq + off_prev)) | ((col >= w) & (col < 2 * w)) \
        | ((col >= 2 * w) & (col < 3 * w) & (col - 2 * w + off_next <= qq)) | (col >= 3 * w)
    s = jnp.where(valid, s, NEG_BIG)
    sink = jnp.concatenate([jnp.full((w, 1), sink_ref[j] * LOG2E, F32) for j in range(nh)], axis=0)
    m = jnp.maximum(sink, jnp.max(s, axis=-1, keepdims=True))
    e = jnp.exp2(s - m)
    denom = jnp.exp2(sink - m) + jnp.sum(e, axis=-1, keepdims=True)
    o = _dot(e.astype(BF16), v) / denom
    left = _lane((w, LANES)) < HEAD_DIM
    for p in range(nh // 2):
        o_ref[0, p] = jnp.where(left, o[2 * p * w:(2 * p + 1) * w],
                                o[(2 * p + 1) * w:(2 * p + 2) * w]).astype(BF16)


def _window_attention(q, k_all, v_all, sinks, n_lat):
    b, nh, _, _ = q.shape
    w = C_WINDOW
    nb = n_lat // w
    n_ctx = k_all.shape[2] - n_lat
    assert n_lat % n_ctx == 0 and w & (w - 1) == 0
    prev = pl.BlockSpec((1, 1, w, LANES), lambda i, j: (i, 0, jnp.maximum(j - 1, 0), 0))
    cur = pl.BlockSpec((1, 1, w, LANES), lambda i, j: (i, 0, j, 0))
    nxt = pl.BlockSpec((1, 1, w, LANES), lambda i, j: (i, 0, jnp.minimum(j + 1, nb - 1), 0))
    ctx = pl.BlockSpec((1, 1, n_ctx, LANES), lambda i, j: (i, 0, n_lat // n_ctx, 0))
    return pl.pallas_call(
        _window_kernel,
        grid=(b, nb),
        in_specs=[pl.BlockSpec(memory_space=pltpu.SMEM),
                  pl.BlockSpec((1, nh, w, LANES), lambda i, j: (i, 0, j, 0)),
                  prev, cur, nxt, ctx, prev, cur, nxt, ctx],
        out_specs=pl.BlockSpec((1, nh // 2, w, LANES), lambda i, j: (i, 0, j, 0)),
        out_shape=jax.ShapeDtypeStruct((b, nh // 2, n_lat, LANES), BF16),
        compiler_params=_cparams(("parallel", "parallel")),
        name="window_attention",
    )(sinks, q, k_all, k_all, k_all, k_all, v_all, v_all, v_all, v_all)


def _mixed_residual(x_ref, oa_ref, ob_ref, oc_ref, mod, wout_ref):
    att = jnp.concatenate([oa_ref[0, 0], oa_ref[0, 1], oa_ref[0, 2], ob_ref[0, 0], ob_ref[0, 1],
                           oc_ref[0, 0], oc_ref[0, 1], oc_ref[0, 2]], axis=-1)
    return x_ref[0] + mod[2:3] * _dot(att, wout_ref[...])


def _ffn_dense_kernel(x_ref, oa_ref, ob_ref, oc_ref, mod_ref, wout_ref, g_ref, w1_ref, w3_ref, w2_ref,
                      o_ref, *, n_chunks):
    mod = mod_ref[0]
    x1 = _mixed_residual(x_ref, oa_ref, ob_ref, oc_ref, mod, wout_ref)
    hf = ((_rms(x1) * g_ref[...]) * (1.0 + mod[4:5]) + mod[3:4]).astype(BF16)
    fc = w1_ref.shape[1] // n_chunks
    y = jnp.zeros(x1.shape, F32)
    for c in range(n_chunks):
        a = _dot(hf, w1_ref[:, c * fc:(c + 1) * fc])
        gate = _dot(hf, w3_ref[:, c * fc:(c + 1) * fc])
        y = y + _dot((_silu(a) * gate).astype(BF16), w2_ref[c * fc:(c + 1) * fc, :])
    o_ref[0] = x1 + mod[5:6] * y


def _resident(shape):
    return pl.BlockSpec(shape, lambda i, j: (0,) * len(shape), pipeline_mode=pl.Buffered(1))


def _ffn_dense(x, oa, ob, oc, mod, per_batch_mod, lw, tm):
    b, t, d = x.shape
    f = lw["w1"].shape[1]
    mod_map = (lambda i, j: (i, 0, 0)) if per_batch_mod else (lambda i, j: (0, 0, 0))

    def att(n):
        return pl.BlockSpec((1, n, tm, LANES), lambda i, j: (i, 0, j, 0))

    return pl.pallas_call(
        functools.partial(_ffn_dense_kernel, n_chunks=2),
        grid=(b, t // tm),
        in_specs=[pl.BlockSpec((1, tm, d), lambda i, j: (i, j, 0)), att(3), att(2), att(3),
                  pl.BlockSpec((1, 6, d), mod_map),
                  _resident((d, d)), _resident((1, d)),
                  _resident((d, f)), _resident((d, f)), _resident((f, d))],
        out_specs=pl.BlockSpec((1, tm, d), lambda i, j: (i, j, 0)),
        out_shape=jax.ShapeDtypeStruct((b, t, d), F32),
        compiler_params=_cparams(("parallel", "parallel")),
        name="ffn_dense",
    )(x, oa, ob, oc, mod, lw["w_out"], lw["g_ffn"], lw["w1"], lw["w3"], lw["w2"])


def _route_kernel(x_ref, oa_ref, ob_ref, oc_ref, mod_ref, wout_ref, g_ref, wr_ref,
                  x1_ref, hf_ref, ids_ref, gates_ref):
    mod = mod_ref[0]
    x1 = _mixed_residual(x_ref, oa_ref, ob_ref, oc_ref, mod, wout_ref)
    hf = (_rms(x1) * g_ref[...]) * (1.0 + mod[4:5]) + mod[3:4]
    x1_ref[0] = x1
    hf_ref[0] = hf
    logits = _dot3(hf, wr_ref[...])
    lane = _lane(logits.shape)
    logits = jnp.where(lane < N_EXPERTS, logits, NEG_BIG)
    m1 = jnp.max(logits, axis=-1, keepdims=True)
    i1 = jnp.min(jnp.where(logits == m1, lane, LANES), axis=-1, keepdims=True)
    rest = jnp.where(lane == i1, NEG_BIG, logits)
    m2 = jnp.max(rest, axis=-1, keepdims=True)
    i2 = jnp.min(jnp.where(rest == m2, lane, LANES), axis=-1, keepdims=True)
    e2 = jnp.exp(m2 - m1)
    g1 = 1.0 / (1.0 + e2)
    ids_ref[0] = jnp.where(lane == 0, i1, jnp.where(lane == 1, i2, 0))
    gates_ref[0] = jnp.where(lane == 0, g1, jnp.where(lane == 1, e2 * g1, 0.0))


def _route(x, oa, ob, oc, mod, lw, tm):
    b, t, d = x.shape

    def att(n):
        return pl.BlockSpec((1, n, tm, LANES), lambda i, j: (i, 0, j, 0))

    def const(shape):
        return pl.BlockSpec(shape, lambda i, j: (0,) * len(shape))

    tok = pl.BlockSpec((1, tm, d), lambda i, j: (i, j, 0))
    meta = pl.BlockSpec((1, tm, LANES), lambda i, j: (i, j, 0))
    return pl.pallas_call(
        _route_kernel,
        grid=(b, t // tm),
        in_specs=[tok, att(3), att(2), att(3), pl.BlockSpec((1, 6, d), lambda i, j: (i, 0, 0)),
                  const((d, d)), const((1, d)), const((d, LANES))],
        out_specs=[tok, tok, meta, meta],
        out_shape=[jax.ShapeDtypeStruct((b, t, d), F32), jax.ShapeDtypeStruct((b, t, d), F32),
                   jax.ShapeDtypeStruct((b, t, LANES), jnp.int32),
                   jax.ShapeDtypeStruct((b, t, LANES), F32)],
        compiler_params=_cparams(("parallel", "parallel")),
        name="moe_route",
    )(x, oa, ob, oc, mod, lw["w_out"], lw["g_ffn"], lw["w_router"])


def _row_copy(src_ref, src_row, dst_ref, dst_row, sem):
    return pltpu.make_async_copy(src_ref.at[pl.ds(src_row, 1)], dst_ref.at[pl.ds(dst_row, 1)], sem)


def _load_positions(pos_ref, pos_smem, sem):
    cp = pltpu.make_async_copy(pos_ref.at[pl.program_id(0)], pos_smem, sem)
    cp.start()
    cp.wait()


def _dispatch_kernel(pos_ref, hf_ref, xs_in_ref, xs_ref, pos_smem, pos_sem, sem):
    del xs_in_ref
    rows = pos_smem.shape[0] // 2
    base = pl.program_id(0) * rows
    _load_positions(pos_ref, pos_smem, pos_sem)

    def issue(r, carry):
        _row_copy(hf_ref, base + r, xs_ref, pos_smem[r], sem).start()
        _row_copy(hf_ref, base + r, xs_ref, pos_smem[rows + r], sem).start()
        return carry

    lax.fori_loop(0, rows, issue, 0)

    def drain(r, carry):
        _row_copy(hf_ref, base + r, xs_ref, pos_smem[r], sem).wait()
        _row_copy(hf_ref, base + r, xs_ref, pos_smem[rows + r], sem).wait()
        return carry

    lax.fori_loop(0, rows, drain, 0)


def _dispatch(pos, hf, n_sorted):
    n, d = hf.shape
    n_tiles, two_rows = pos.shape
    return pl.pallas_call(
        _dispatch_kernel,
        grid=(n_tiles,),
        in_specs=[pl.BlockSpec(memory_space=pltpu.VMEM), pl.BlockSpec(memory_space=pl.ANY),
                  pl.BlockSpec(memory_space=pl.ANY)],
        out_specs=pl.BlockSpec(memory_space=pl.ANY),
        out_shape=jax.ShapeDtypeStruct((n_sorted, d), F32),
        scratch_shapes=[pltpu.SMEM((two_rows,), jnp.int32), pltpu.SemaphoreType.DMA(()),
                        pltpu.SemaphoreType.DMA(())],
        input_output_aliases={2: 0},
        compiler_params=_cparams(("arbitrary",)),
        name="moe_dispatch",
    )(pos, hf, jnp.zeros((n_sorted, d), F32))


def _experts_kernel(te_ref, tv_ref, xs_ref, w1_ref, w3_ref, w2_ref, y_ref, xb_scr, acc_scr):
    i, f = pl.program_id(0), pl.program_id(1)

    @pl.when(f == 0)
    def _():
        xb_scr[...] = xs_ref[...].astype(BF16)
        acc_scr[...] = jnp.zeros(acc_scr.shape, F32)

    @pl.when(tv_ref[i] > 0)
    def _():
        xb = xb_scr[...]
        a = _dot(xb, w1_ref[...])
        gate = _dot(xb, w3_ref[...])
        acc_scr[...] += _dot((_silu(a) * gate).astype(BF16), w2_ref[...])

    @pl.when(f == pl.num_programs(1) - 1)
    def _():
        y_ref[...] = acc_scr[...]


def _experts(tile_expert, tile_valid, xs, lw, tm, tf):
    n_sorted, d = xs.shape
    f = lw["w1"].shape[2]
    grid_spec = pltpu.PrefetchScalarGridSpec(
        num_scalar_prefetch=2,
        grid=(n_sorted // tm, f // tf),
        in_specs=[pl.BlockSpec((tm, d), lambda i, k, te, tv: (i, 0)),
                  pl.BlockSpec((None, d, tf), lambda i, k, te, tv: (te[i], 0, k)),
                  pl.BlockSpec((None, d, tf), lambda i, k, te, tv: (te[i], 0, k)),
                  pl.BlockSpec((None, tf, d), lambda i, k, te, tv: (te[i], k, 0))],
        out_specs=pl.BlockSpec((tm, d), lambda i, k, te, tv: (i, 0)),
        scratch_shapes=[pltpu.VMEM((tm, d), BF16), pltpu.VMEM((tm, d), F32)])
    return pl.pallas_call(
        _experts_kernel,
        grid_spec=grid_spec,
        out_shape=jax.ShapeDtypeStruct((n_sorted, d), F32),
        compiler_params=_cparams(("parallel", "arbitrary")),
        name="moe_experts",
    )(tile_expert, tile_valid, xs, lw["w1"], lw["w3"], lw["w2"])


def _combine_kernel(pos_ref, y_ref, x1_ref, gates_ref, mod_ref, gfin_ref, o_ref,
                    pos_smem, y0_scr, y1_scr, pos_sem, sem):
    rows = y0_scr.shape[0]
    _load_positions(pos_ref, pos_smem, pos_sem)

    def issue(r, carry):
        _row_copy(y_ref, pos_smem[r], y0_scr, r, sem).start()
        _row_copy(y_ref, pos_smem[rows + r], y1_scr, r, sem).start()
        return carry

    lax.fori_loop(0, rows, issue, 0)

    def drain(r, carry):
        _row_copy(y_ref, pos_smem[r], y0_scr, r, sem).wait()
        _row_copy(y_ref, pos_smem[rows + r], y1_scr, r, sem).wait()
        return carry

    lax.fori_loop(0, rows, drain, 0)
    gates = gates_ref[...]
    y = gates[:, 0:1] * y0_scr[...] + gates[:, 1:2] * y1_scr[...]
    o_ref[...] = _rms(x1_ref[...] + mod_ref[0, 5:6] * y) * gfin_ref[...]


def _combine(pos, y, x1, gates, mod, gfin, tokens_per_batch):
    n, d = x1.shape
    n_tiles, two_rows = pos.shape
    rows = two_rows // 2
    tok = pl.BlockSpec((rows, d), lambda i: (i, 0))
    return pl.pallas_call(
        _combine_kernel,
        grid=(n_tiles,),
        in_specs=[pl.BlockSpec(memory_space=pltpu.VMEM), pl.BlockSpec(memory_space=pl.ANY), tok,
                  pl.BlockSpec((rows, LANES), lambda i: (i, 0)),
                  pl.BlockSpec((1, 6, d), lambda i: (i * rows // tokens_per_batch, 0, 0)),
                  pl.BlockSpec((1, d), lambda i: (0, 0))],
        out_specs=tok,
        out_shape=jax.ShapeDtypeStruct((n, d), F32),
        scratch_shapes=[pltpu.SMEM((two_rows,), jnp.int32), pltpu.VMEM((rows, d), F32),
                        pltpu.VMEM((rows, d), F32), pltpu.SemaphoreType.DMA(()),
                        pltpu.SemaphoreType.DMA(())],
        compiler_params=_cparams(("arbitrary",)),
        name="moe_combine",
    )(pos, y, x1, gates, mod, gfin)


def _dispatch_plan(ids, tm, rows):
    n = ids.shape[0]
    flat = ids.reshape(-1)
    onehot = (flat[None, :] == jnp.arange(N_EXPERTS)[:, None]).astype(jnp.int32)
    csum = jnp.cumsum(onehot, axis=1)
    rank = jnp.sum(csum * onehot, axis=0) - 1
    counts = csum[:, -1]
    padded = (counts + tm - 1) // tm * tm
    ends = jnp.cumsum(padded)
    pos = (ends - padded)[flat] + rank
    n_sorted = 2 * n + N_EXPERTS * tm
    starts = jnp.arange(n_sorted // tm, dtype=jnp.int32) * tm
    tile_expert = jnp.sum((starts[:, None] >= ends[None, :]).astype(jnp.int32), axis=1)
    tile_valid = (starts < ends[-1]).astype(jnp.int32)
    last_expert = tile_expert[jnp.maximum(ends[-1] // tm - 1, 0)]
    tile_expert = jnp.where(tile_valid > 0, tile_expert, last_expert)
    pos = pos.reshape(n // rows, rows, 2).transpose(0, 2, 1).reshape(n // rows, 2 * rows)
    return pos.astype(jnp.int32), tile_expert, tile_valid, n_sorted


def _ffn_moe(x, oa, ob, oc, mod, lw, gfin):
    b, t, d = x.shape
    n = b * t
    assert t % ROW_TILE == 0
    x1, hf, ids, gates = _route(x, oa, ob, oc, mod, lw, 512)
    pos, tile_expert, tile_valid, n_sorted = _dispatch_plan(ids.reshape(n, LANES)[:, :2], MOE_TM,
                                                            ROW_TILE)
    xs = _dispatch(pos, hf.reshape(n, d), n_sorted)
    y = _experts(tile_expert, tile_valid, xs, lw, MOE_TM, MOE_TF)
    out = _combine(pos, y, x1.reshape(n, d), gates.reshape(n, LANES), mod, gfin, t)
    return out.reshape(b, t, d)


def _rope_tables(n_lat):
    rows = n_lat // GRID_W
    row = jnp.repeat(jnp.arange(rows, dtype=F32), GRID_W)
    col = jnp.tile(jnp.arange(GRID_W, dtype=F32), rows)

    def table(rot_dim):
        nf = rot_dim // 4
        inv = ROPE_THETA ** (-jnp.arange(nf, dtype=F32) / nf)
        ang_r, ang_c = row[:, None] * inv, col[:, None] * inv
        ang = jnp.concatenate([ang_r, ang_r, ang_c, ang_c], axis=-1)
        sign = jnp.tile(jnp.concatenate([-jnp.ones(nf, F32), jnp.ones(nf, F32)]), 2)
        return jnp.cos(ang), jnp.sin(ang) * sign

    cos_h, sin_h = table(HEAD_DIM)
    cos_r, sin_r = table(B_ROPE)
    one = jnp.ones((n_lat, B_NOPE), F32)
    pad = LANES - B_NOPE - B_ROPE
    cos_b = jnp.concatenate([one, cos_r, jnp.ones((n_lat, pad), F32)], axis=-1)
    sin_b = jnp.concatenate([0 * one, sin_r, jnp.zeros((n_lat, pad), F32)], axis=-1)
    return (jnp.tile(cos_h, (1, 2)), jnp.tile(sin_h, (1, 2)), cos_b, sin_b)


def _permute_heads(w, axis):
    shape = w.shape
    w = w.reshape(shape[:axis] + (6, HEAD_DIM) + shape[axis + 1:])
    w = jnp.take(w, jnp.array(HEAD_PERM), axis=axis)
    return w.reshape(shape)


def _layer_weights(i, w_in, norm_ffn, a_q_norm, a_k_norm, b_q_norm, b_kv_norm, w_uq, w_ukv, w_out):
    d = w_in.shape[1]
    wi = w_in[i]
    qk = A_HEADS * HEAD_DIM
    kv = A_KV * HEAD_DIM
    b0 = A_COLS
    c0 = A_COLS + B_COLS
    k_rope = jnp.pad(wi[:, b0 + B_Q_RANK + B_KV_RANK:c0], ((0, 0), (B_NOPE, LANES - B_NOPE - B_ROPE)))
    w_in_p = jnp.concatenate([
        _permute_heads(wi[:, :qk], 1), wi[:, qk:A_COLS],
        wi[:, b0:b0 + B_Q_RANK + B_KV_RANK], k_rope,
        _permute_heads(wi[:, c0:c0 + qk], 1), wi[:, c0 + qk:]], axis=1).astype(BF16)
    w_uq_p = jnp.pad(w_uq[i].reshape(B_Q_RANK, B_HEADS, B_NOPE + B_ROPE),
                     ((0, 0), (0, 0), (0, LANES - B_NOPE - B_ROPE))).reshape(B_Q_RANK, B_HEADS * LANES)
    ukv = w_ukv[i].reshape(B_KV_RANK, B_HEADS, B_NOPE + B_V)
    k_up = jnp.pad(ukv[:, :, :B_NOPE], ((0, 0), (0, 0), (0, LANES - B_NOPE))).reshape(B_KV_RANK, -1)
    v_up = ukv[:, :, B_NOPE:].reshape(B_KV_RANK, -1)
    wo = w_out[i]
    w_out_p = jnp.concatenate([_permute_heads(wo[:qk], 0), wo[qk:qk + B_HEADS * B_V],
                               _permute_heads(wo[qk + B_HEADS * B_V:], 0)], axis=0).astype(BF16)
    blk = (jnp.arange(LANES)[:, None] // HEAD_DIM) == (jnp.arange(LANES)[None, :] // HEAD_DIM)
    del kv, d
    return {
        "w_in": w_in_p,
        "m2": (blk.astype(F32) / HEAD_DIM).astype(BF16),
        "a_q_gain": jnp.tile(a_q_norm[i], 2)[None], "a_k_gain": jnp.tile(a_k_norm[i], 2)[None],
        "b_q_gain": b_q_norm[i][None], "b_kv_gain": b_kv_norm[i][None],
        "w_uq": w_uq_p.astype(BF16),
        "w_ukv": jnp.concatenate([k_up, v_up], axis=1).astype(BF16),
        "w_out": w_out_p,
        "g_ffn": norm_ffn[i][None],
    }


def kernel(x, c, ctx, c_ctx, w_mod, b_mod, norm_mix, norm_ffn, w_in, a_q_norm, a_k_norm, b_q_norm,
           b_kv_norm, w_uq, w_ukv, c_sinks, w_out, w1_dense, w3_dense, w2_dense, w_router, w1_moe,
           w3_moe, w2_moe, final_norm):
    bsz, n_lat, d = x.shape
    n_ctx = ctx.shape[1]
    depth = w_mod.shape[0]
    mod_rows = 16
    cc = jnp.concatenate([c, c_ctx[None], jnp.zeros((mod_rows - bsz - 1, d), F32)], axis=0)
    mods = _modulation(cc, w_mod, b_mod)
    tables = _rope_tables(n_lat)
    no_sink = jnp.zeros((A_HEADS,), F32)
    xc = ctx
    for i in range(depth):
        last = i == depth - 1
        mod_lat = mods[i, :bsz].reshape(bsz, 6, d)
        mod_ctx = mods[i, bsz:bsz + 1].reshape(1, 6, d)
        lw = _layer_weights(i, w_in, norm_ffn, a_q_norm, a_k_norm, b_q_norm, b_kv_norm, w_uq, w_ukv,
                            w_out)
        g_mix = norm_mix[i][None]
        lat = _in_projection(x, mod_lat, True, g_mix, lw, tables, True, 512)
        cx = _in_projection(xc, mod_ctx, False, g_mix, lw, tables, False, n_ctx)
        qa, ka, va, qb, kb, vb, qc, kc, vc = lat
        cqa, cka, cva, cqb, ckb, cvb, cqc, ckc, cvc = cx

        def both(u, w):
            return jnp.concatenate([u, w], axis=2)

        sinks = jnp.take(c_sinks[i], jnp.array(HEAD_PERM))
        oa = _flash(qa, both(ka, cka), both(va, cva), no_sink, (0,) * 6, (0,) * 6, False, 512)
        ob = _flash(qb, both(kb, ckb), both(vb, cvb), no_sink, (0, 1, 2, 3), (0, 0, 1, 1), False, 512)
        oc = _window_attention(qc, both(kc, ckc), both(vc, cvc), sinks, n_lat)
        if not last:
            coa = _flash(cqa, cka, cva, no_sink, (0,) * 6, (0,) * 6, False, n_ctx)
            cob = _flash(cqb, ckb, cvb, no_sink, (0, 1, 2, 3), (0, 0, 1, 1), False, n_ctx)
            cvc_t = jnp.swapaxes(cvc.reshape(bsz, 1, n_ctx // FLASH_TK, FLASH_TK, LANES), 3, 4)
            coc = _flash(cqc, ckc, cvc_t, sinks, (0,) * 6, (0,) * 6, True, n_ctx)
        j = i // 2
        if i % 2 == 0:
            lw.update(w1=w1_dense[j].astype(BF16), w3=w3_dense[j].astype(BF16),
                      w2=w2_dense[j].astype(BF16))
            x_new = _ffn_dense(x, oa, ob, oc, mod_lat, True, lw, 512)
            if not last:
                xc = _ffn_dense(xc, coa, cob, coc, mod_ctx, False, lw, n_ctx)
            x = x_new
        else:
            assert last, "the expert channel mixer is fused with the final norm"
            lw.update(w1=w1_moe[j].astype(BF16), w3=w3_moe[j].astype(BF16), w2=w2_moe[j].astype(BF16),
                      w_router=jnp.pad(w_router[j], ((0, 0), (0, LANES - N_EXPERTS))))
            x = _ffn_moe(x, oa, ob, oc, mod_lat, lw, final_norm[None])
    return x
```

```python
import functools

import jax
import jax.numpy as jnp
from jax import lax
from jax.experimental import pallas as pl
from jax.experimental.pallas import tpu as pltpu

F32 = jnp.float32
BF16 = jnp.bfloat16

LANES = 128
HEAD_DIM = 64
GRID_W = 64
ROPE_THETA = 10000.0
EPS = 1e-6
NEG_BIG = -1e30
A_HEADS, A_KV = 6, 2
B_HEADS, B_Q_RANK, B_KV_RANK, B_NOPE, B_ROPE, B_V = 4, 256, 128, 64, 32, 64
C_HEADS, C_KV, C_WINDOW = 6, 2, 128
N_EXPERTS = 8
A_COLS = (A_HEADS + 2 * A_KV) * HEAD_DIM
B_COLS = B_Q_RANK + B_KV_RANK + B_ROPE
VMEM_LIMIT = 56 * 1024 * 1024
LOG2E = 1.4426950408889634
FLASH_TK = 512
MOE_TM = 1024
MOE_TF = 896
ROW_TILE = 1024

HEAD_PERM = tuple((j % 2) * 3 + j // 2 for j in range(6))


def _cparams(sem):
    return pltpu.CompilerParams(dimension_semantics=sem, vmem_limit_bytes=VMEM_LIMIT)


def _rms(x):
    return x * lax.rsqrt(jnp.mean(x * x, axis=-1, keepdims=True) + EPS)


def _dot(a, b):
    return jnp.dot(a, b, preferred_element_type=F32)


def _dot_nt(a, b):
    return lax.dot_general(a, b, (((1,), (1,)), ((), ())), preferred_element_type=F32)


def _split(a):
    hi = a.astype(BF16)
    return hi, (a - hi.astype(F32)).astype(BF16)


def _dot3(a, b):
    ah, al = _split(a)
    bh, bl = _split(b)
    return _dot(ah, bh) + (_dot(ah, bl) + _dot(al, bh))


def _silu(a):
    return a * jax.nn.sigmoid(a)


def _lane(shape):
    return lax.broadcasted_iota(jnp.int32, shape, len(shape) - 1)


def _mod_kernel(c_ref, w_ref, b_ref, o_ref):
    o_ref[...] = _dot3(_silu(c_ref[...]), w_ref[...]) + b_ref[...]


def _modulation(cc, w_mod, b_mod):
    depth, d, n = w_mod.shape
    rows = cc.shape[0]
    tn = 1536
    return pl.pallas_call(
        _mod_kernel,
        grid=(depth, n // tn),
        in_specs=[pl.BlockSpec((rows, d), lambda i, j: (0, 0)),
                  pl.BlockSpec((None, d, tn), lambda i, j: (i, 0, j)),
                  pl.BlockSpec((None, 1, tn), lambda i, j: (i, 0, j))],
        out_specs=pl.BlockSpec((None, rows, tn), lambda i, j: (i, 0, j)),
        out_shape=jax.ShapeDtypeStruct((depth, rows, n), F32),
        compiler_params=_cparams(("arbitrary", "arbitrary")),
        name="modulation",
    )(cc, w_mod, b_mod.reshape(depth, 1, n))


def _rope(t, cos, sin_signed, half):
    first = (_lane(t.shape) & (2 * half - 1)) < half
    rot = jnp.where(first, pltpu.roll(t, LANES - half, 1), pltpu.roll(t, half, 1))
    return t * cos + rot * sin_signed


def _store_pair(ref, p, t):
    left = _lane(t.shape) < HEAD_DIM
    ref[0, 2 * p] = jnp.where(left, t, 0.0).astype(BF16)
    ref[0, 2 * p + 1] = jnp.where(left, 0.0, t).astype(BF16)


def _proj_kernel(x_ref, mod_ref, g_ref, win_ref, m2_ref, gaq_ref, gak_ref, gbq_ref, gbkv_ref,
                 wuq_ref, wukv_ref, cosa_ref, sina_ref, cosb_ref, sinb_ref,
                 qa_ref, ka_ref, va_ref, qb_ref, kb_ref, vb_ref, qc_ref, kc_ref, vc_ref,
                 *, use_rope):
    mod = mod_ref[0]
    h = _rms(x_ref[0]) * g_ref[...]
    h = h * (1.0 + mod[1:2]) + mod[0:1]
    proj = _dot(h.astype(BF16), win_ref[...])

    def chunk(i):
        return proj[:, i * LANES:(i + 1) * LANES]

    def head_norm(t, gain):
        ms = _dot((t * t).astype(BF16), m2_ref[...])
        return t * lax.rsqrt(ms + EPS) * gain

    def rope_a(t):
        return _rope(t, cosa_ref[...], sina_ref[...], HEAD_DIM // 4) if use_rope else t

    def rope_b(t):
        return _rope(t, cosb_ref[...], sinb_ref[...], B_ROPE // 4) if use_rope else t

    scale = HEAD_DIM ** -0.5 * LOG2E
    for p in range(3):
        _store_pair(qa_ref, p, rope_a(head_norm(chunk(p), gaq_ref[...])) * scale)
    ka_ref[0, 0] = rope_a(head_norm(chunk(3), gak_ref[...])).astype(BF16)
    va_ref[0, 0] = chunk(4).astype(BF16)
    cq = _rms(proj[:, 5 * LANES:7 * LANES]) * gbq_ref[...]
    qb = _dot(cq.astype(BF16), wuq_ref[...])
    ckv = _rms(chunk(7)) * gbkv_ref[...]
    kvb = _dot(ckv.astype(BF16), wukv_ref[...])
    k_rope = rope_b(chunk(8))
    scale_b = (B_NOPE + B_ROPE) ** -0.5 * LOG2E
    for hd in range(B_HEADS):
        qb_ref[0, hd] = (rope_b(qb[:, hd * LANES:(hd + 1) * LANES]) * scale_b).astype(BF16)
        kb_ref[0, hd] = (kvb[:, hd * LANES:(hd + 1) * LANES] + k_rope).astype(BF16)
    for p in range(2):
        vb_ref[0, p] = kvb[:, (B_HEADS + p) * LANES:(B_HEADS + p + 1) * LANES].astype(BF16)
    for p in range(3):
        _store_pair(qc_ref, p, rope_a(chunk(9 + p)) * scale)
    kc_ref[0, 0] = rope_a(chunk(12)).astype(BF16)
    vc_ref[0, 0] = chunk(13).astype(BF16)


def _in_projection(x, mod, per_batch_mod, g, lw, tables, use_rope, tm):
    b, t, d = x.shape
    ncol = lw["w_in"].shape[1]
    heads = (6, 1, 1, 4, 4, 2, 6, 1, 1)

    def const(shape):
        return pl.BlockSpec(shape, lambda i, j: (0,) * len(shape))

    mod_map = (lambda i, j: (i, 0, 0)) if per_batch_mod else (lambda i, j: (0, 0, 0))
    tab = pl.BlockSpec((tm, LANES), lambda i, j: (j, 0))
    return pl.pallas_call(
        functools.partial(_proj_kernel, use_rope=use_rope),
        grid=(b, t // tm),
        in_specs=[pl.BlockSpec((1, tm, d), lambda i, j: (i, j, 0)),
                  pl.BlockSpec((1, 6, d), mod_map),
                  const((1, d)), const((d, ncol)), const((LANES, LANES)),
                  const((1, LANES)), const((1, LANES)), const((1, B_Q_RANK)), const((1, B_KV_RANK)),
                  const((B_Q_RANK, 4 * LANES)), const((B_KV_RANK, 6 * LANES)),
                  tab, tab, tab, tab],
        out_specs=[pl.BlockSpec((1, n, tm, LANES), lambda i, j: (i, 0, j, 0)) for n in heads],
        out_shape=[jax.ShapeDtypeStruct((b, n, t, LANES), BF16) for n in heads],
        compiler_params=_cparams(("parallel", "parallel")),
        name="in_projection",
    )(x, mod, g, lw["w_in"], lw["m2"], lw["a_q_gain"], lw["a_k_gain"], lw["b_q_gain"],
      lw["b_kv_gain"], lw["w_uq"], lw["w_ukv"], *tables)


def _flash_kernel(sink_ref, q_ref, k_ref, v_ref, o_ref, *, kidx, vidx, tk, use_sink):
    nh, tq = q_ref.shape[1], q_ref.shape[2]
    t_keys = k_ref.shape[2]
    nfull, tail = t_keys // tk, t_keys % tk
    left = _lane((tq, LANES)) < HEAD_DIM
    held = None
    for j in range(nh):
        q = q_ref[0, j]

        def step(start, size, carry, j=j, q=q):
            m, l, acc = carry
            k = k_ref[0, kidx[j], pl.ds(start, size), :]
            v = v_ref[0, vidx[j], pl.ds(start, size), :]
            s = _dot_nt(q, k)
            m_new = jnp.maximum(m, jnp.max(s, axis=-1, keepdims=True))
            alpha = jnp.exp2(m - m_new)
            p = jnp.exp2(s - m_new)
            l = alpha * l + jnp.sum(p, axis=-1, keepdims=True)
            acc = alpha * acc + _dot(p.astype(BF16), v)
            return m_new, l, acc

        if use_sink:
            carry = (jnp.full((tq, 1), sink_ref[j] * LOG2E, F32), jnp.ones((tq, 1), F32))
        else:
            carry = (jnp.full((tq, 1), NEG_BIG, F32), jnp.zeros((tq, 1), F32))
        carry = carry + (jnp.zeros((tq, LANES), F32),)
        if nfull:
            carry = lax.fori_loop(
                0, nfull, lambda c, cr: step(pl.multiple_of(c * tk, tk), tk, cr), carry)
        if tail:
            carry = step(nfull * tk, tail, carry)
        out = carry[2] / carry[1]
        if j % 2 == 0:
            held = out
        else:
            o_ref[0, j // 2] = jnp.where(left, held, out).astype(BF16)


def _flash(q, k, v, sinks, kidx, vidx, use_sink, tq):
    b, nh, t, _ = q.shape
    nk, tkeys = k.shape[1], k.shape[2]
    nv = v.shape[1]
    return pl.pallas_call(
        functools.partial(_flash_kernel, kidx=kidx, vidx=vidx, tk=FLASH_TK, use_sink=use_sink),
        grid=(b, t // tq),
        in_specs=[pl.BlockSpec(memory_space=pltpu.SMEM),
                  pl.BlockSpec((1, nh, tq, LANES), lambda i, j: (i, 0, j, 0)),
                  pl.BlockSpec((1, nk, tkeys, LANES), lambda i, j: (i, 0, 0, 0)),
                  pl.BlockSpec((1, nv, tkeys, LANES), lambda i, j: (i, 0, 0, 0))],
        out_specs=pl.BlockSpec((1, nh // 2, tq, LANES), lambda i, j: (i, 0, j, 0)),
        out_shape=jax.ShapeDtypeStruct((b, nh // 2, t, LANES), BF16),
        compiler_params=_cparams(("parallel", "parallel")),
        name="flash_attention",
    )(sinks, q, k, v)


def _window_kernel(sink_ref, q_ref, kp_ref, kc_ref, kn_ref, kx_ref, vp_ref, vc_ref, vn_ref, vx_ref,
                   o_ref):
    nh, w = q_ref.shape[1], q_ref.shape[2]
    n_ctx = kx_ref.shape[2]
    n, nb = pl.program_id(1), pl.num_programs(1)
    q = q_ref[0].reshape(nh * w, LANES)
    k = jnp.concatenate([kp_ref[0, 0], kc_ref[0, 0], kn_ref[0, 0], kx_ref[0, 0]], axis=0)
    v = jnp.concatenate([vp_ref[0, 0], vc_ref[0, 0], vn_ref[0, 0], vx_ref[0, 0]], axis=0)
    s = _dot_nt(q, k)
    shape = (nh * w, 3 * w + n_ctx)
    qq = lax.broadcasted_iota(jnp.int32, shape, 0) & (w - 1)
    col = lax.broadcasted_iota(jnp.int32, shape, 1)
    never = 4 * w
    off_prev = jnp.where(n >= 1, 0, never)
    off_next = jnp.where(n <= nb - 2, 0, never)
    valid = ((col < w) & (col >= qq + off_prev)) | ((col >= w) & (col < 2 * w)) \
        | ((col >= 2 * w) & (col < 3 * w) & (col - 2 * w + off_next <= qq)) | (col >= 3 * w)
    s = jnp.where(valid, s, NEG_BIG)
    sink = jnp.concatenate([jnp.full((w, 1), sink_ref[j] * LOG2E, F32) for j in range(nh)], axis=0)
    m = jnp.maximum(sink, jnp.max(s, axis=-1, keepdims=True))
    e = jnp.exp2(s - m)
    denom = jnp.exp2(sink - m) + jnp.sum(e, axis=-1, keepdims=True)
    o = _dot(e.astype(BF16), v) / denom
    left = _lane((w, LANES)) < HEAD_DIM
    for p in range(nh // 2):
        o_ref[0, p] = jnp.where(left, o[2 * p * w:(2 * p + 1) * w],
                                o[(2 * p + 1) * w:(2 * p + 2) * w]).astype(BF16)


def _window_attention(q, k_all, v_all, sinks, n_lat):
    b, nh, _, _ = q.shape
    w = C_WINDOW
    nb = n_lat // w
    n_ctx = k_all.shape[2] - n_lat
    assert n_lat % n_ctx == 0 and w & (w - 1) == 0
    prev = pl.BlockSpec((1, 1, w, LANES), lambda i, j: (i, 0, jnp.maximum(j - 1, 0), 0))
    cur = pl.BlockSpec((1, 1, w, LANES), lambda i, j: (i, 0, j, 0))
    nxt = pl.BlockSpec((1, 1, w, LANES), lambda i, j: (i, 0, jnp.minimum(j + 1, nb - 1), 0))
    ctx = pl.BlockSpec((1, 1, n_ctx, LANES), lambda i, j: (i, 0, n_lat // n_ctx, 0))
    return pl.pallas_call(
        _window_kernel,
        grid=(b, nb),
        in_specs=[pl.BlockSpec(memory_space=pltpu.SMEM),
                  pl.BlockSpec((1, nh, w, LANES), lambda i, j: (i, 0, j, 0)),
                  prev, cur, nxt, ctx, prev, cur, nxt, ctx],
        out_specs=pl.BlockSpec((1, nh // 2, w, LANES), lambda i, j: (i, 0, j, 0)),
        out_shape=jax.ShapeDtypeStruct((b, nh // 2, n_lat, LANES), BF16),
        compiler_params=_cparams(("parallel", "parallel")),
        name="window_attention",
    )(sinks, q, k_all, k_all, k_all, k_all, v_all, v_all, v_all, v_all)


def _mixed_residual(x_ref, oa_ref, ob_ref, oc_ref, mod, wout_ref):
    att = jnp.concatenate([oa_ref[0, 0], oa_ref[0, 1], oa_ref[0, 2], ob_ref[0, 0], ob_ref[0, 1],
                           oc_ref[0, 0], oc_ref[0, 1], oc_ref[0, 2]], axis=-1)
    return x_ref[0] + mod[2:3] * _dot(att, wout_ref[...])


def _ffn_dense_kernel(x_ref, oa_ref, ob_ref, oc_ref, mod_ref, wout_ref, g_ref, w1_ref, w3_ref, w2_ref,
                      o_ref, *, n_chunks):
    mod = mod_ref[0]
    x1 = _mixed_residual(x_ref, oa_ref, ob_ref, oc_ref, mod, wout_ref)
    hf = ((_rms(x1) * g_ref[...]) * (1.0 + mod[4:5]) + mod[3:4]).astype(BF16)
    fc = w1_ref.shape[1] // n_chunks
    y = jnp.zeros(x1.shape, F32)
    for c in range(n_chunks):
        a = _dot(hf, w1_ref[:, c * fc:(c + 1) * fc])
        gate = _dot(hf, w3_ref[:, c * fc:(c + 1) * fc])
        y = y + _dot((_silu(a) * gate).astype(BF16), w2_ref[c * fc:(c + 1) * fc, :])
    o_ref[0] = x1 + mod[5:6] * y


def _resident(shape):
    return pl.BlockSpec(shape, lambda i, j: (0,) * len(shape), pipeline_mode=pl.Buffered(1))


def _ffn_dense(x, oa, ob, oc, mod, per_batch_mod, lw, tm):
    b, t, d = x.shape
    f = lw["w1"].shape[1]
    mod_map = (lambda i, j: (i, 0, 0)) if per_batch_mod else (lambda i, j: (0, 0, 0))

    def att(n):
        return pl.BlockSpec((1, n, tm, LANES), lambda i, j: (i, 0, j, 0))

    return pl.pallas_call(
        functools.partial(_ffn_dense_kernel, n_chunks=2),
        grid=(b, t // tm),
        in_specs=[pl.BlockSpec((1, tm, d), lambda i, j: (i, j, 0)), att(3), att(2), att(3),
                  pl.BlockSpec((1, 6, d), mod_map),
                  _resident((d, d)), _resident((1, d)),
                  _resident((d, f)), _resident((d, f)), _resident((f, d))],
        out_specs=pl.BlockSpec((1, tm, d), lambda i, j: (i, j, 0)),
        out_shape=jax.ShapeDtypeStruct((b, t, d), F32),
        compiler_params=_cparams(("parallel", "parallel")),
        name="ffn_dense",
    )(x, oa, ob, oc, mod, lw["w_out"], lw["g_ffn"], lw["w1"], lw["w3"], lw["w2"])


def _route_kernel(x_ref, oa_ref, ob_ref, oc_ref, mod_ref, wout_ref, g_ref, wr_ref,
                  x1_ref, hf_ref, ids_ref, gates_ref):
    mod = mod_ref[0]
    x1 = _mixed_residual(x_ref, oa_ref, ob_ref, oc_ref, mod, wout_ref)
    hf = (_rms(x1) * g_ref[...]) * (1.0 + mod[4:5]) + mod[3:4]
    x1_ref[0] = x1
    hf_ref[0] = hf
    logits = _dot3(hf, wr_ref[...])
    lane = _lane(logits.shape)
    logits = jnp.where(lane < N_EXPERTS, logits, NEG_BIG)
    m1 = jnp.max(logits, axis=-1, keepdims=True)
    i1 = jnp.min(jnp.where(logits == m1, lane, LANES), axis=-1, keepdims=True)
    rest = jnp.where(lane == i1, NEG_BIG, logits)
    m2 = jnp.max(rest, axis=-1, keepdims=True)
    i2 = jnp.min(jnp.where(rest == m2, lane, LANES), axis=-1, keepdims=True)
    e2 = jnp.exp(m2 - m1)
    g1 = 1.0 / (1.0 + e2)
    ids_ref[0] = jnp.where(lane == 0, i1, jnp.where(lane == 1, i2, 0))
    gates_ref[0] = jnp.where(lane == 0, g1, jnp.where(lane == 1, e2 * g1, 0.0))


def _route(x, oa, ob, oc, mod, lw, tm):
    b, t, d = x.shape

    def att(n):
        return pl.BlockSpec((1, n, tm, LANES), lambda i, j: (i, 0, j, 0))

    def const(shape):
        return pl.BlockSpec(shape, lambda i, j: (0,) * len(shape))

    tok = pl.BlockSpec((1, tm, d), lambda i, j: (i, j, 0))
    meta = pl.BlockSpec((1, tm, LANES), lambda i, j: (i, j, 0))
    return pl.pallas_call(
        _route_kernel,
        grid=(b, t // tm),
        in_specs=[tok, att(3), att(2), att(3), pl.BlockSpec((1, 6, d), lambda i, j: (i, 0, 0)),
                  const((d, d)), const((1, d)), const((d, LANES))],
        out_specs=[tok, tok, meta, meta],
        out_shape=[jax.ShapeDtypeStruct((b, t, d), F32), jax.ShapeDtypeStruct((b, t, d), F32),
                   jax.ShapeDtypeStruct((b, t, LANES), jnp.int32),
                   jax.ShapeDtypeStruct((b, t, LANES), F32)],
        compiler_params=_cparams(("parallel", "parallel")),
        name="moe_route",
    )(x, oa, ob, oc, mod, lw["w_out"], lw["g_ffn"], lw["w_router"])


def _row_copy(src_ref, src_row, dst_ref, dst_row, sem):
    return pltpu.make_async_copy(src_ref.at[pl.ds(src_row, 1)], dst_ref.at[pl.ds(dst_row, 1)], sem)


def _load_indices(idx_ref, tile, idx_smem, sem):
    cp = pltpu.make_async_copy(idx_ref.at[tile], idx_smem, sem)
    cp.start()
    cp.wait()


def _wait_rows(src_ref, dst_ref, sem):
    pltpu.make_async_copy(src_ref.at[pl.ds(0, dst_ref.shape[0])], dst_ref, sem).wait()


def _experts_kernel(te_ref, tv_ref, src_ref, hf_ref, w1_ref, w3_ref, w2_ref, y_ref,
                    src_smem, xg_scr, xb_scr, acc_scr, src_sem, sems):
    i, f = pl.program_id(0), pl.program_id(1)
    tm = xb_scr.shape[0]

    def gather(tile):
        slot = tile % 2
        _load_indices(src_ref, tile, src_smem, src_sem)

        def issue(r, carry):
            _row_copy(hf_ref, src_smem[r], xg_scr.at[slot], r, sems.at[slot]).start()
            return carry

        lax.fori_loop(0, tm, issue, 0)

    @pl.when((f == 0) & (i == 0))
    def _():
        gather(0)

    @pl.when((f == 0) & (tv_ref[i] > 0))
    def _():
        slot = i % 2
        _wait_rows(hf_ref, xg_scr.at[slot], sems.at[slot])
        xb_scr[...] = xg_scr[slot].astype(BF16)

    @pl.when(f == 0)
    def _():
        acc_scr[...] = jnp.zeros(acc_scr.shape, F32)
        nxt = jnp.minimum(i + 1, pl.num_programs(0) - 1)

        @pl.when((i + 1 < pl.num_programs(0)) & (tv_ref[nxt] > 0))
        def _():
            gather(i + 1)

    @pl.when(tv_ref[i] > 0)
    def _():
        xb = xb_scr[...]
        a = _dot(xb, w1_ref[...])
        gate = _dot(xb, w3_ref[...])
        acc_scr[...] += _dot((_silu(a) * gate).astype(BF16), w2_ref[...])

    @pl.when(f == pl.num_programs(1) - 1)
    def _():
        y_ref[...] = acc_scr[...]


def _experts(tile_expert, tile_valid, src, hf, lw, tf):
    n_tiles, tm = src.shape
    d = hf.shape[1]
    f = lw["w1"].shape[2]
    grid_spec = pltpu.PrefetchScalarGridSpec(
        num_scalar_prefetch=2,
        grid=(n_tiles, f // tf),
        in_specs=[pl.BlockSpec(memory_space=pltpu.VMEM), pl.BlockSpec(memory_space=pl.ANY),
                  pl.BlockSpec((None, d, tf), lambda i, k, te, tv: (te[i], 0, k)),
                  pl.BlockSpec((None, d, tf), lambda i, k, te, tv: (te[i], 0, k)),
                  pl.BlockSpec((None, tf, d), lambda i, k, te, tv: (te[i], k, 0))],
        out_specs=pl.BlockSpec((tm, d), lambda i, k, te, tv: (i, 0)),
        scratch_shapes=[pltpu.SMEM((tm,), jnp.int32), pltpu.VMEM((2, tm, d), F32),
                        pltpu.VMEM((tm, d), BF16), pltpu.VMEM((tm, d), F32),
                        pltpu.SemaphoreType.DMA(()), pltpu.SemaphoreType.DMA((2,))])
    return pl.pallas_call(
        _experts_kernel,
        grid_spec=grid_spec,
        out_shape=jax.ShapeDtypeStruct((n_tiles * tm, d), F32),
        compiler_params=_cparams(("arbitrary", "arbitrary")),
        name="moe_experts",
    )(tile_expert, tile_valid, src, hf, lw["w1"], lw["w3"], lw["w2"])


def _combine_kernel(pos_ref, y_ref, x1_ref, gates_ref, mod_ref, gfin_ref, o_ref,
                    pos_smem, y0_scr, y1_scr, pos_sem, sem):
    rows = y0_scr.shape[0]
    _load_indices(pos_ref, pl.program_id(0), pos_smem, pos_sem)

    def issue(r, carry):
        _row_copy(y_ref, pos_smem[r], y0_scr, r, sem).start()
        _row_copy(y_ref, pos_smem[rows + r], y1_scr, r, sem).start()
        return carry

    lax.fori_loop(0, rows, issue, 0)
    _wait_rows(y_ref, y0_scr, sem)
    _wait_rows(y_ref, y1_scr, sem)
    gates = gates_ref[...]
    y = gates[:, 0:1] * y0_scr[...] + gates[:, 1:2] * y1_scr[...]
    o_ref[...] = _rms(x1_ref[...] + mod_ref[0, 5:6] * y) * gfin_ref[...]


def _combine(pos, y, x1, gates, mod, gfin, tokens_per_batch):
    n, d = x1.shape
    n_tiles, two_rows = pos.shape
    rows = two_rows // 2
    tok = pl.BlockSpec((rows, d), lambda i: (i, 0))
    return pl.pallas_call(
        _combine_kernel,
        grid=(n_tiles,),
        in_specs=[pl.BlockSpec(memory_space=pltpu.VMEM), pl.BlockSpec(memory_space=pl.ANY), tok,
                  pl.BlockSpec((rows, LANES), lambda i: (i, 0)),
                  pl.BlockSpec((1, 6, d), lambda i: (i * rows // tokens_per_batch, 0, 0)),
                  pl.BlockSpec((1, d), lambda i: (0, 0))],
        out_specs=tok,
        out_shape=jax.ShapeDtypeStruct((n, d), F32),
        scratch_shapes=[pltpu.SMEM((two_rows,), jnp.int32), pltpu.VMEM((rows, d), F32),
                        pltpu.VMEM((rows, d), F32), pltpu.SemaphoreType.DMA(()),
                        pltpu.SemaphoreType.DMA(())],
        compiler_params=_cparams(("arbitrary",)),
        name="moe_combine",
    )(pos, y, x1, gates, mod, gfin)


def _dispatch_plan(ids, tm, rows):
    n = ids.shape[0]
    flat = ids.reshape(-1)
    onehot = (flat[None, :] == jnp.arange(N_EXPERTS)[:, None]).astype(jnp.int32)
    csum = jnp.cumsum(onehot, axis=1)
    rank = jnp.sum(csum * onehot, axis=0) - 1
    counts = csum[:, -1]
    padded = (counts + tm - 1) // tm * tm
    ends = jnp.cumsum(padded)
    pos = (ends - padded)[flat] + rank
    n_sorted = 2 * n + N_EXPERTS * tm
    starts = jnp.arange(n_sorted // tm, dtype=jnp.int32) * tm
    tile_expert = jnp.sum((starts[:, None] >= ends[None, :]).astype(jnp.int32), axis=1)
    tile_valid = (starts < ends[-1]).astype(jnp.int32)
    last_expert = tile_expert[jnp.maximum(ends[-1] // tm - 1, 0)]
    tile_expert = jnp.where(tile_valid > 0, tile_expert, last_expert)
    order = jnp.argsort(flat, stable=True).astype(jnp.int32)
    first = jnp.cumsum(counts) - counts
    row = jnp.arange(n_sorted, dtype=jnp.int32)
    row_expert = jnp.repeat(tile_expert, tm)
    r = row - (ends - padded)[row_expert]
    live = (r < counts[row_expert]) & (jnp.repeat(tile_valid, tm) > 0)
    src = jnp.where(live, order[jnp.clip(first[row_expert] + r, 0, 2 * n - 1)] // 2, 0)
    pos = pos.reshape(n // rows, rows, 2).transpose(0, 2, 1).reshape(n // rows, 2 * rows)
    return pos.astype(jnp.int32), src.reshape(n_sorted // tm, tm), tile_expert, tile_valid


def _ffn_moe(x, oa, ob, oc, mod, lw, gfin):
    b, t, d = x.shape
    n = b * t
    assert t % ROW_TILE == 0
    x1, hf, ids, gates = _route(x, oa, ob, oc, mod, lw, 512)
    pos, src, tile_expert, tile_valid = _dispatch_plan(ids.reshape(n, LANES)[:, :2], MOE_TM, ROW_TILE)
    y = _experts(tile_expert, tile_valid, src, hf.reshape(n, d), lw, MOE_TF)
    out = _combine(pos, y, x1.reshape(n, d), gates.reshape(n, LANES), mod, gfin, t)
    return out.reshape(b, t, d)


def _rope_tables(n_lat):
    rows = n_lat // GRID_W
    row = jnp.repeat(jnp.arange(rows, dtype=F32), GRID_W)
    col = jnp.tile(jnp.arange(GRID_W, dtype=F32), rows)

    def table(rot_dim):
        nf = rot_dim // 4
        inv = ROPE_THETA ** (-jnp.arange(nf, dtype=F32) / nf)
        ang_r, ang_c = row[:, None] * inv, col[:, None] * inv
        ang = jnp.concatenate([ang_r, ang_r, ang_c, ang_c], axis=-1)
        sign = jnp.tile(jnp.concatenate([-jnp.ones(nf, F32), jnp.ones(nf, F32)]), 2)
        return jnp.cos(ang), jnp.sin(ang) * sign

    cos_h, sin_h = table(HEAD_DIM)
    cos_r, sin_r = table(B_ROPE)
    one = jnp.ones((n_lat, B_NOPE), F32)
    pad = LANES - B_NOPE - B_ROPE
    cos_b = jnp.concatenate([one, cos_r, jnp.ones((n_lat, pad), F32)], axis=-1)
    sin_b = jnp.concatenate([0 * one, sin_r, jnp.zeros((n_lat, pad), F32)], axis=-1)
    return (jnp.tile(cos_h, (1, 2)), jnp.tile(sin_h, (1, 2)), cos_b, sin_b)


def _permute_heads(w, axis):
    shape = w.shape
    w = w.reshape(shape[:axis] + (6, HEAD_DIM) + shape[axis + 1:])
    w = jnp.take(w, jnp.array(HEAD_PERM), axis=axis)
    return w.reshape(shape)


def _layer_weights(i, w_in, norm_ffn, a_q_norm, a_k_norm, b_q_norm, b_kv_norm, w_uq, w_ukv, w_out):
    d = w_in.shape[1]
    wi = w_in[i]
    qk = A_HEADS * HEAD_DIM
    kv = A_KV * HEAD_DIM
    b0 = A_COLS
    c0 = A_COLS + B_COLS
    k_rope = jnp.pad(wi[:, b0 + B_Q_RANK + B_KV_RANK:c0], ((0, 0), (B_NOPE, LANES - B_NOPE - B_ROPE)))
    w_in_p = jnp.concatenate([
        _permute_heads(wi[:, :qk], 1), wi[:, qk:A_COLS],
        wi[:, b0:b0 + B_Q_RANK + B_KV_RANK], k_rope,
        _permute_heads(wi[:, c0:c0 + qk], 1), wi[:, c0 + qk:]], axis=1).astype(BF16)
    w_uq_p = jnp.pad(w_uq[i].reshape(B_Q_RANK, B_HEADS, B_NOPE + B_ROPE),
                     ((0, 0), (0, 0), (0, LANES - B_NOPE - B_ROPE))).reshape(B_Q_RANK, B_HEADS * LANES)
    ukv = w_ukv[i].reshape(B_KV_RANK, B_HEADS, B_NOPE + B_V)
    k_up = jnp.pad(ukv[:, :, :B_NOPE], ((0, 0), (0, 0), (0, LANES - B_NOPE))).reshape(B_KV_RANK, -1)
    v_up = ukv[:, :, B_NOPE:].reshape(B_KV_RANK, -1)
    wo = w_out[i]
    w_out_p = jnp.concatenate([_permute_heads(wo[:qk], 0), wo[qk:qk + B_HEADS * B_V],
                               _permute_heads(wo[qk + B_HEADS * B_V:], 0)], axis=0).astype(BF16)
    blk = (jnp.arange(LANES)[:, None] // HEAD_DIM) == (jnp.arange(LANES)[None, :] // HEAD_DIM)
    del kv, d
    return {
        "w_in": w_in_p,
        "m2": (blk.astype(F32) / HEAD_DIM).astype(BF16),
        "a_q_gain": jnp.tile(a_q_norm[i], 2)[None], "a_k_gain": jnp.tile(a_k_norm[i], 2)[None],
        "b_q_gain": b_q_norm[i][None], "b_kv_gain": b_kv_norm[i][None],
        "w_uq": w_uq_p.astype(BF16),
        "w_ukv": jnp.concatenate([k_up, v_up], axis=1).astype(BF16),
        "w_out": w_out_p,
        "g_ffn": norm_ffn[i][None],
    }


def kernel(x, c, ctx, c_ctx, w_mod, b_mod, norm_mix, norm_ffn, w_in, a_q_norm, a_k_norm, b_q_norm,
           b_kv_norm, w_uq, w_ukv, c_sinks, w_out, w1_dense, w3_dense, w2_dense, w_router, w1_moe,
           w3_moe, w2_moe, final_norm):
    bsz, n_lat, d = x.shape
    n_ctx = ctx.shape[1]
    depth = w_mod.shape[0]
    mod_rows = 16
    cc = jnp.concatenate([c, c_ctx[None], jnp.zeros((mod_rows - bsz - 1, d), F32)], axis=0)
    mods = _modulation(cc, w_mod, b_mod)
    tables = _rope_tables(n_lat)
    no_sink = jnp.zeros((A_HEADS,), F32)
    xc = ctx
    for i in range(depth):
        last = i == depth - 1
        mod_lat = mods[i, :bsz].reshape(bsz, 6, d)
        mod_ctx = mods[i, bsz:bsz + 1].reshape(1, 6, d)
        lw = _layer_weights(i, w_in, norm_ffn, a_q_norm, a_k_norm, b_q_norm, b_kv_norm, w_uq, w_ukv,
                            w_out)
        g_mix = norm_mix[i][None]
        lat = _in_projection(x, mod_lat, True, g_mix, lw, tables, True, 512)
        cx = _in_projection(xc, mod_ctx, False, g_mix, lw, tables, False, n_ctx)
        qa, ka, va, qb, kb, vb, qc, kc, vc = lat
        cqa, cka, cva, cqb, ckb, cvb, cqc, ckc, cvc = cx

        def both(u, w):
            return jnp.concatenate([u, w], axis=2)

        sinks = jnp.take(c_sinks[i], jnp.array(HEAD_PERM))
        oa = _flash(qa, both(ka, cka), both(va, cva), no_sink, (0,) * 6, (0,) * 6, False, 512)
        ob = _flash(qb, both(kb, ckb), both(vb, cvb), no_sink, (0, 1, 2, 3), (0, 0, 1, 1), False, 512)
        oc = _window_attention(qc, both(kc, ckc), both(vc, cvc), sinks, n_lat)
        if not last:
            coa = _flash(cqa, cka, cva, no_sink, (0,) * 6, (0,) * 6, False, n_ctx)
            cob = _flash(cqb, ckb, cvb, no_sink, (0, 1, 2, 3), (0, 0, 1, 1), False, n_ctx)
            coc = _flash(cqc, ckc, cvc, sinks, (0,) * 6, (0,) * 6, True, n_ctx)
        j = i // 2
        if i % 2 == 0:
            lw.update(w1=w1_dense[j].astype(BF16), w3=w3_dense[j].astype(BF16),
                      w2=w2_dense[j].astype(BF16))
            x_new = _ffn_dense(x, oa, ob, oc, mod_lat, True, lw, 512)
            if not last:
                xc = _ffn_dense(xc, coa, cob, coc, mod_ctx, False, lw, n_ctx)
            x = x_new
        else:
            assert last, "the expert channel mixer is fused with the final norm"
            lw.update(w1=w1_moe[j].astype(BF16), w3=w3_moe[j].astype(BF16), w2=w2_moe[j].astype(BF16),
                      w_router=jnp.pad(w_router[j], ((0, 0), (0, LANES - N_EXPERTS))))
            x = _ffn_moe(x, oa, ob, oc, mod_lat, lw, final_norm[None])
    return x
```

```python
import functools

import jax
import jax.numpy as jnp
from jax import lax
from jax.experimental import pallas as pl
from jax.experimental.pallas import tpu as pltpu

F32 = jnp.float32
BF16 = jnp.bfloat16

LANES = 128
HEAD_DIM = 64
GRID_W = 64
ROPE_THETA = 10000.0
EPS = 1e-6
NEG_BIG = -1e30
A_HEADS, A_KV = 6, 2
B_HEADS, B_Q_RANK, B_KV_RANK, B_NOPE, B_ROPE, B_V = 4, 256, 128, 64, 32, 64
C_HEADS, C_KV, C_WINDOW = 6, 2, 128
N_EXPERTS = 8
A_COLS = (A_HEADS + 2 * A_KV) * HEAD_DIM
B_COLS = B_Q_RANK + B_KV_RANK + B_ROPE
VMEM_LIMIT = 56 * 1024 * 1024
LOG2E = 1.4426950408889634
FLASH_TK = 2048
ISSUE_UNROLL = 8
MOE_TM = 1024
MOE_TF = 896
ROW_TILE = 1024

HEAD_PERM = tuple((j % 2) * 3 + j // 2 for j in range(6))


def _cparams(sem):
    return pltpu.CompilerParams(dimension_semantics=sem, vmem_limit_bytes=VMEM_LIMIT)


def _rms(x):
    return x * lax.rsqrt(jnp.mean(x * x, axis=-1, keepdims=True) + EPS)


def _dot(a, b):
    return jnp.dot(a, b, preferred_element_type=F32)


def _dot_nt(a, b):
    return lax.dot_general(a, b, (((1,), (1,)), ((), ())), preferred_element_type=F32)


def _split(a):
    hi = a.astype(BF16)
    return hi, (a - hi.astype(F32)).astype(BF16)


def _dot3(a, b):
    ah, al = _split(a)
    bh, bl = _split(b)
    return _dot(ah, bh) + (_dot(ah, bl) + _dot(al, bh))


def _silu(a):
    return a * jax.nn.sigmoid(a)


def _lane(shape):
    return lax.broadcasted_iota(jnp.int32, shape, len(shape) - 1)


def _mod_kernel(c_ref, w_ref, b_ref, o_ref):
    o_ref[...] = _dot3(_silu(c_ref[...]), w_ref[...]) + b_ref[...]


def _modulation(cc, w_mod, b_mod):
    depth, d, n = w_mod.shape
    rows = cc.shape[0]
    tn = 1536
    return pl.pallas_call(
        _mod_kernel,
        grid=(depth, n // tn),
        in_specs=[pl.BlockSpec((rows, d), lambda i, j: (0, 0)),
                  pl.BlockSpec((None, d, tn), lambda i, j: (i, 0, j)),
                  pl.BlockSpec((None, 1, tn), lambda i, j: (i, 0, j))],
        out_specs=pl.BlockSpec((None, rows, tn), lambda i, j: (i, 0, j)),
        out_shape=jax.ShapeDtypeStruct((depth, rows, n), F32),
        compiler_params=_cparams(("arbitrary", "arbitrary")),
        name="modulation",
    )(cc, w_mod, b_mod.reshape(depth, 1, n))


def _rope(t, cos, sin_signed, half):
    first = (_lane(t.shape) & (2 * half - 1)) < half
    rot = jnp.where(first, pltpu.roll(t, LANES - half, 1), pltpu.roll(t, half, 1))
    return t * cos + rot * sin_signed


def _store_pair(ref, p, t):
    left = _lane(t.shape) < HEAD_DIM
    ref[0, 2 * p] = jnp.where(left, t, 0.0).astype(BF16)
    ref[0, 2 * p + 1] = jnp.where(left, 0.0, t).astype(BF16)


def _proj_kernel(x_ref, mod_ref, g_ref, win_ref, m2_ref, gaq_ref, gak_ref, gbq_ref, gbkv_ref,
                 wuq_ref, wukv_ref, cosa_ref, sina_ref, cosb_ref, sinb_ref,
                 qa_ref, ka_ref, va_ref, qb_ref, kb_ref, vb_ref, qc_ref, kc_ref, vc_ref,
                 *, use_rope):
    mod = mod_ref[0]
    h = _rms(x_ref[0]) * g_ref[...]
    h = h * (1.0 + mod[1:2]) + mod[0:1]
    proj = _dot(h.astype(BF16), win_ref[...])

    def chunk(i):
        return proj[:, i * LANES:(i + 1) * LANES]

    def head_norm(t, gain):
        ms = _dot((t * t).astype(BF16), m2_ref[...])
        return t * lax.rsqrt(ms + EPS) * gain

    def rope_a(t):
        return _rope(t, cosa_ref[...], sina_ref[...], HEAD_DIM // 4) if use_rope else t

    def rope_b(t):
        return _rope(t, cosb_ref[...], sinb_ref[...], B_ROPE // 4) if use_rope else t

    scale = HEAD_DIM ** -0.5 * LOG2E
    for p in range(3):
        _store_pair(qa_ref, p, rope_a(head_norm(chunk(p), gaq_ref[...])) * scale)
    ka_ref[0, 0] = rope_a(head_norm(chunk(3), gak_ref[...])).astype(BF16)
    va_ref[0, 0] = chunk(4).astype(BF16)
    cq = _rms(proj[:, 5 * LANES:7 * LANES]) * gbq_ref[...]
    qb = _dot(cq.astype(BF16), wuq_ref[...])
    ckv = _rms(chunk(7)) * gbkv_ref[...]
    kvb = _dot(ckv.astype(BF16), wukv_ref[...])
    k_rope = rope_b(chunk(8))
    scale_b = (B_NOPE + B_ROPE) ** -0.5 * LOG2E
    for hd in range(B_HEADS):
        qb_ref[0, hd] = (rope_b(qb[:, hd * LANES:(hd + 1) * LANES]) * scale_b).astype(BF16)
        kb_ref[0, hd] = (kvb[:, hd * LANES:(hd + 1) * LANES] + k_rope).astype(BF16)
    for p in range(2):
        vb_ref[0, p] = kvb[:, (B_HEADS + p) * LANES:(B_HEADS + p + 1) * LANES].astype(BF16)
    for p in range(3):
        _store_pair(qc_ref, p, rope_a(chunk(9 + p)) * scale)
    kc_ref[0, 0] = rope_a(chunk(12)).astype(BF16)
    vc_ref[0, 0] = chunk(13).astype(BF16)


def _in_projection(x, mod, per_batch_mod, g, lw, tables, use_rope, tm):
    b, t, d = x.shape
    ncol = lw["w_in"].shape[1]
    heads = (6, 1, 1, 4, 4, 2, 6, 1, 1)

    def const(shape):
        return pl.BlockSpec(shape, lambda i, j: (0,) * len(shape))

    mod_map = (lambda i, j: (i, 0, 0)) if per_batch_mod else (lambda i, j: (0, 0, 0))
    tab = pl.BlockSpec((tm, LANES), lambda i, j: (j, 0))
    return pl.pallas_call(
        functools.partial(_proj_kernel, use_rope=use_rope),
        grid=(b, t // tm),
        in_specs=[pl.BlockSpec((1, tm, d), lambda i, j: (i, j, 0)),
                  pl.BlockSpec((1, 6, d), mod_map),
                  const((1, d)), const((d, ncol)), const((LANES, LANES)),
                  const((1, LANES)), const((1, LANES)), const((1, B_Q_RANK)), const((1, B_KV_RANK)),
                  const((B_Q_RANK, 4 * LANES)), const((B_KV_RANK, 6 * LANES)),
                  tab, tab, tab, tab],
        out_specs=[pl.BlockSpec((1, n, tm, LANES), lambda i, j: (i, 0, j, 0)) for n in heads],
        out_shape=[jax.ShapeDtypeStruct((b, n, t, LANES), BF16) for n in heads],
        compiler_params=_cparams(("parallel", "parallel")),
        name="in_projection",
    )(x, mod, g, lw["w_in"], lw["m2"], lw["a_q_gain"], lw["a_k_gain"], lw["b_q_gain"],
      lw["b_kv_gain"], lw["w_uq"], lw["w_ukv"], *tables)


def _flash_kernel(sink_ref, q_ref, k_ref, v_ref, o_ref, *, kidx, vidx, tk, use_sink):
    nh, tq = q_ref.shape[1], q_ref.shape[2]
    t_keys = k_ref.shape[2]
    nfull, tail = t_keys // tk, t_keys % tk
    left = _lane((tq, LANES)) < HEAD_DIM
    for pair in range(nh // 2):
        heads = (2 * pair, 2 * pair + 1)

        def step(start, size, carry, heads=heads):
            new = []
            for j, (m, l, acc) in zip(heads, carry):
                k = k_ref[0, kidx[j], pl.ds(start, size), :]
                v = v_ref[0, vidx[j], pl.ds(start, size), :]
                s = _dot_nt(q_ref[0, j], k)
                m_new = jnp.maximum(m, jnp.max(s, axis=-1, keepdims=True))
                alpha = jnp.exp2(m - m_new)
                p = jnp.exp2(s - m_new)
                l = alpha * l + jnp.sum(p, axis=-1, keepdims=True)
                acc = alpha * acc + _dot(p.astype(BF16), v)
                new.append((m_new, l, acc))
            return tuple(new)

        def init(j):
            if use_sink:
                m, l = jnp.full((tq, 1), sink_ref[j] * LOG2E, F32), jnp.ones((tq, 1), F32)
            else:
                m, l = jnp.full((tq, 1), NEG_BIG, F32), jnp.zeros((tq, 1), F32)
            return m, l, jnp.zeros((tq, LANES), F32)

        carry = tuple(init(j) for j in heads)
        if nfull:
            carry = lax.fori_loop(
                0, nfull, lambda c, cr: step(pl.multiple_of(c * tk, tk), tk, cr), carry)
        if tail:
            carry = step(nfull * tk, tail, carry)
        even, odd = (c[2] / c[1] for c in carry)
        o_ref[0, pair] = jnp.where(left, even, odd).astype(BF16)


def _flash(q, k, v, sinks, kidx, vidx, use_sink, tq):
    b, nh, t, _ = q.shape
    nk, tkeys = k.shape[1], k.shape[2]
    nv = v.shape[1]
    return pl.pallas_call(
        functools.partial(_flash_kernel, kidx=kidx, vidx=vidx, tk=FLASH_TK, use_sink=use_sink),
        grid=(b, t // tq),
        in_specs=[pl.BlockSpec(memory_space=pltpu.SMEM),
                  pl.BlockSpec((1, nh, tq, LANES), lambda i, j: (i, 0, j, 0)),
                  pl.BlockSpec((1, nk, tkeys, LANES), lambda i, j: (i, 0, 0, 0)),
                  pl.BlockSpec((1, nv, tkeys, LANES), lambda i, j: (i, 0, 0, 0))],
        out_specs=pl.BlockSpec((1, nh // 2, tq, LANES), lambda i, j: (i, 0, j, 0)),
        out_shape=jax.ShapeDtypeStruct((b, nh // 2, t, LANES), BF16),
        compiler_params=_cparams(("parallel", "parallel")),
        name="flash_attention",
    )(sinks, q, k, v)


def _window_block(sink, q, ks, vs, has_prev, has_next, w):
    k = jnp.concatenate(ks, axis=0)
    v = jnp.concatenate(vs, axis=0)
    s = _dot_nt(q, k)
    qq = lax.broadcasted_iota(jnp.int32, s.shape, 0) & (w - 1)
    col = lax.broadcasted_iota(jnp.int32, s.shape, 1)
    never = 4 * w
    off_prev = jnp.where(has_prev, 0, never)
    off_next = jnp.where(has_next, 0, never)
    valid = ((col < w) & (col >= qq + off_prev)) | ((col >= w) & (col < 2 * w)) \
        | ((col >= 2 * w) & (col < 3 * w) & (col - 2 * w + off_next <= qq)) | (col >= 3 * w)
    s = jnp.where(valid, s, NEG_BIG)
    m = jnp.maximum(sink, jnp.max(s, axis=-1, keepdims=True))
    e = jnp.exp2(s - m)
    denom = jnp.exp2(sink - m) + jnp.sum(e, axis=-1, keepdims=True)
    return _dot(e.astype(BF16), v) / denom


def _window_kernel(sink_ref, q_ref, kp_ref, kc_ref, kn_ref, kx_ref, vp_ref, vc_ref, vn_ref, vx_ref,
                   o_ref):
    nh, w = q_ref.shape[1], q_ref.shape[2] // 2
    i, steps = pl.program_id(1), pl.num_programs(1)
    sink = jnp.concatenate([jnp.full((w, 1), sink_ref[j] * LOG2E, F32) for j in range(nh)], axis=0)
    k_lo, k_hi = kc_ref[0, 0, :w], kc_ref[0, 0, w:]
    v_lo, v_hi = vc_ref[0, 0, :w], vc_ref[0, 0, w:]
    kx, vx = kx_ref[0, 0], vx_ref[0, 0]
    q_first = q_ref[0, :, :w].reshape(nh * w, LANES)
    q_second = q_ref[0, :, w:].reshape(nh * w, LANES)
    first = _window_block(sink, q_first, [kp_ref[0, 0], k_lo, k_hi, kx],
                          [vp_ref[0, 0], v_lo, v_hi, vx], i >= 1, True, w)
    second = _window_block(sink, q_second, [k_lo, k_hi, kn_ref[0, 0], kx],
                           [v_lo, v_hi, vn_ref[0, 0], vx], True, i <= steps - 2, w)
    left = _lane((w, LANES)) < HEAD_DIM
    for p in range(nh // 2):
        for half, o in enumerate((first, second)):
            o_ref[0, p, half * w:(half + 1) * w] = jnp.where(
                left, o[2 * p * w:(2 * p + 1) * w], o[(2 * p + 1) * w:(2 * p + 2) * w]).astype(BF16)


def _window_attention(q, k_all, v_all, sinks, n_lat):
    b, nh, _, _ = q.shape
    w = C_WINDOW
    nb = n_lat // w
    n_ctx = k_all.shape[2] - n_lat
    assert n_lat % n_ctx == 0 and w & (w - 1) == 0 and nb % 2 == 0 and n_ctx % (2 * w) == 0
    prev = pl.BlockSpec((1, 1, w, LANES), lambda i, j: (i, 0, jnp.maximum(2 * j - 1, 0), 0))
    cur = pl.BlockSpec((1, 1, 2 * w, LANES), lambda i, j: (i, 0, j, 0))
    nxt = pl.BlockSpec((1, 1, w, LANES), lambda i, j: (i, 0, jnp.minimum(2 * j + 2, nb - 1), 0))
    ctx = pl.BlockSpec((1, 1, n_ctx, LANES), lambda i, j: (i, 0, n_lat // n_ctx, 0))
    return pl.pallas_call(
        _window_kernel,
        grid=(b, nb // 2),
        in_specs=[pl.BlockSpec(memory_space=pltpu.SMEM),
                  pl.BlockSpec((1, nh, 2 * w, LANES), lambda i, j: (i, 0, j, 0)),
                  prev, cur, nxt, ctx, prev, cur, nxt, ctx],
        out_specs=pl.BlockSpec((1, nh // 2, 2 * w, LANES), lambda i, j: (i, 0, j, 0)),
        out_shape=jax.ShapeDtypeStruct((b, nh // 2, n_lat, LANES), BF16),
        compiler_params=_cparams(("parallel", "parallel")),
        name="window_attention",
    )(sinks, q, k_all, k_all, k_all, k_all, v_all, v_all, v_all, v_all)


def _mixed_residual(x_ref, oa_ref, ob_ref, oc_ref, mod, wout_ref):
    att = jnp.concatenate([oa_ref[0, 0], oa_ref[0, 1], oa_ref[0, 2], ob_ref[0, 0], ob_ref[0, 1],
                           oc_ref[0, 0], oc_ref[0, 1], oc_ref[0, 2]], axis=-1)
    return x_ref[0] + mod[2:3] * _dot(att, wout_ref[...])


def _ffn_dense_kernel(x_ref, oa_ref, ob_ref, oc_ref, mod_ref, wout_ref, g_ref, w1_ref, w3_ref, w2_ref,
                      o_ref, *, n_chunks):
    mod = mod_ref[0]
    x1 = _mixed_residual(x_ref, oa_ref, ob_ref, oc_ref, mod, wout_ref)
    hf = ((_rms(x1) * g_ref[...]) * (1.0 + mod[4:5]) + mod[3:4]).astype(BF16)
    fc = w1_ref.shape[1] // n_chunks
    y = jnp.zeros(x1.shape, F32)
    for c in range(n_chunks):
        a = _dot(hf, w1_ref[:, c * fc:(c + 1) * fc])
        gate = _dot(hf, w3_ref[:, c * fc:(c + 1) * fc])
        y = y + _dot((_silu(a) * gate).astype(BF16), w2_ref[c * fc:(c + 1) * fc, :])
    o_ref[0] = x1 + mod[5:6] * y


def _resident(shape):
    return pl.BlockSpec(shape, lambda i, j: (0,) * len(shape), pipeline_mode=pl.Buffered(1))


def _ffn_dense(x, oa, ob, oc, mod, per_batch_mod, lw, tm):
    b, t, d = x.shape
    f = lw["w1"].shape[1]
    mod_map = (lambda i, j: (i, 0, 0)) if per_batch_mod else (lambda i, j: (0, 0, 0))

    def att(n):
        return pl.BlockSpec((1, n, tm, LANES), lambda i, j: (i, 0, j, 0))

    return pl.pallas_call(
        functools.partial(_ffn_dense_kernel, n_chunks=2),
        grid=(b, t // tm),
        in_specs=[pl.BlockSpec((1, tm, d), lambda i, j: (i, j, 0)), att(3), att(2), att(3),
                  pl.BlockSpec((1, 6, d), mod_map),
                  _resident((d, d)), _resident((1, d)),
                  _resident((d, f)), _resident((d, f)), _resident((f, d))],
        out_specs=pl.BlockSpec((1, tm, d), lambda i, j: (i, j, 0)),
        out_shape=jax.ShapeDtypeStruct((b, t, d), F32),
        compiler_params=_cparams(("parallel", "parallel")),
        name="ffn_dense",
    )(x, oa, ob, oc, mod, lw["w_out"], lw["g_ffn"], lw["w1"], lw["w3"], lw["w2"])


def _route_kernel(x_ref, oa_ref, ob_ref, oc_ref, mod_ref, wout_ref, g_ref, wr_ref,
                  x1_ref, hf_ref, ids_ref, gates_ref):
    mod = mod_ref[0]
    x1 = _mixed_residual(x_ref, oa_ref, ob_ref, oc_ref, mod, wout_ref)
    hf = (_rms(x1) * g_ref[...]) * (1.0 + mod[4:5]) + mod[3:4]
    x1_ref[0] = x1
    hf_ref[0] = hf
    logits = _dot3(hf, wr_ref[...])
    lane = _lane(logits.shape)
    logits = jnp.where(lane < N_EXPERTS, logits, NEG_BIG)
    m1 = jnp.max(logits, axis=-1, keepdims=True)
    i1 = jnp.min(jnp.where(logits == m1, lane, LANES), axis=-1, keepdims=True)
    rest = jnp.where(lane == i1, NEG_BIG, logits)
    m2 = jnp.max(rest, axis=-1, keepdims=True)
    i2 = jnp.min(jnp.where(rest == m2, lane, LANES), axis=-1, keepdims=True)
    e2 = jnp.exp(m2 - m1)
    g1 = 1.0 / (1.0 + e2)
    ids_ref[0] = jnp.where(lane == 0, i1, jnp.where(lane == 1, i2, 0))
    gates_ref[0] = jnp.where(lane == 0, g1, jnp.where(lane == 1, e2 * g1, 0.0))


def _route(x, oa, ob, oc, mod, lw, tm):
    b, t, d = x.shape

    def att(n):
        return pl.BlockSpec((1, n, tm, LANES), lambda i, j: (i, 0, j, 0))

    def const(shape):
        return pl.BlockSpec(shape, lambda i, j: (0,) * len(shape))

    tok = pl.BlockSpec((1, tm, d), lambda i, j: (i, j, 0))
    meta = pl.BlockSpec((1, tm, LANES), lambda i, j: (i, j, 0))
    return pl.pallas_call(
        _route_kernel,
        grid=(b, t // tm),
        in_specs=[tok, att(3), att(2), att(3), pl.BlockSpec((1, 6, d), lambda i, j: (i, 0, 0)),
                  const((d, d)), const((1, d)), const((d, LANES))],
        out_specs=[tok, tok, meta, meta],
        out_shape=[jax.ShapeDtypeStruct((b, t, d), F32), jax.ShapeDtypeStruct((b, t, d), F32),
                   jax.ShapeDtypeStruct((b, t, LANES), jnp.int32),
                   jax.ShapeDtypeStruct((b, t, LANES), F32)],
        compiler_params=_cparams(("parallel", "parallel")),
        name="moe_route",
    )(x, oa, ob, oc, mod, lw["w_out"], lw["g_ffn"], lw["w_router"])


def _row_copy(src_ref, src_row, dst_ref, dst_row, sem):
    return pltpu.make_async_copy(src_ref.at[pl.ds(src_row, 1)], dst_ref.at[pl.ds(dst_row, 1)], sem)


def _load_indices(idx_ref, tile, idx_smem, sem):
    cp = pltpu.make_async_copy(idx_ref.at[tile], idx_smem, sem)
    cp.start()
    cp.wait()


def _wait_rows(src_ref, dst_ref, sem):
    pltpu.make_async_copy(src_ref.at[pl.ds(0, dst_ref.shape[0])], dst_ref, sem).wait()


def _experts_kernel(te_ref, tv_ref, src_ref, hf_ref, w1_ref, w3_ref, w2_ref, y_ref,
                    src_smem, xg_scr, xb_scr, acc_scr, src_sem, sems):
    i, f = pl.program_id(0), pl.program_id(1)
    tm = xb_scr.shape[0]

    def gather(tile):
        slot = tile % 2
        _load_indices(src_ref, tile, src_smem, src_sem)

        def issue(r, carry):
            _row_copy(hf_ref, src_smem[r], xg_scr.at[slot], r, sems.at[slot]).start()
            return carry

        lax.fori_loop(0, tm, issue, 0, unroll=ISSUE_UNROLL)

    @pl.when((f == 0) & (i == 0))
    def _():
        gather(0)

    @pl.when((f == 0) & (tv_ref[i] > 0))
    def _():
        slot = i % 2
        _wait_rows(hf_ref, xg_scr.at[slot], sems.at[slot])
        xb_scr[...] = xg_scr[slot].astype(BF16)

    @pl.when(f == 0)
    def _():
        acc_scr[...] = jnp.zeros(acc_scr.shape, F32)
        nxt = jnp.minimum(i + 1, pl.num_programs(0) - 1)

        @pl.when((i + 1 < pl.num_programs(0)) & (tv_ref[nxt] > 0))
        def _():
            gather(i + 1)

    @pl.when(tv_ref[i] > 0)
    def _():
        xb = xb_scr[...]
        a = _dot(xb, w1_ref[...])
        gate = _dot(xb, w3_ref[...])
        acc_scr[...] += _dot((_silu(a) * gate).astype(BF16), w2_ref[...])

    @pl.when(f == pl.num_programs(1) - 1)
    def _():
        y_ref[...] = acc_scr[...]


def _experts(tile_expert, tile_valid, src, hf, lw, tf):
    n_tiles, tm = src.shape
    d = hf.shape[1]
    f = lw["w1"].shape[2]
    grid_spec = pltpu.PrefetchScalarGridSpec(
        num_scalar_prefetch=2,
        grid=(n_tiles, f // tf),
        in_specs=[pl.BlockSpec(memory_space=pltpu.VMEM), pl.BlockSpec(memory_space=pl.ANY),
                  pl.BlockSpec((None, d, tf), lambda i, k, te, tv: (te[i], 0, k)),
                  pl.BlockSpec((None, d, tf), lambda i, k, te, tv: (te[i], 0, k)),
                  pl.BlockSpec((None, tf, d), lambda i, k, te, tv: (te[i], k, 0))],
        out_specs=pl.BlockSpec((tm, d), lambda i, k, te, tv: (i, 0)),
        scratch_shapes=[pltpu.SMEM((tm,), jnp.int32), pltpu.VMEM((2, tm, d), F32),
                        pltpu.VMEM((tm, d), BF16), pltpu.VMEM((tm, d), F32),
                        pltpu.SemaphoreType.DMA(()), pltpu.SemaphoreType.DMA((2,))])
    return pl.pallas_call(
        _experts_kernel,
        grid_spec=grid_spec,
        out_shape=jax.ShapeDtypeStruct((n_tiles * tm, d), F32),
        compiler_params=_cparams(("arbitrary", "arbitrary")),
        name="moe_experts",
    )(tile_expert, tile_valid, src, hf, lw["w1"], lw["w3"], lw["w2"])


def _combine_kernel(pos_ref, y_ref, x1_ref, gates_ref, mod_ref, gfin_ref, o_ref,
                    pos_smem, y0_scr, y1_scr, pos_sem, sem):
    rows = y0_scr.shape[0]
    _load_indices(pos_ref, pl.program_id(0), pos_smem, pos_sem)

    def issue(r, carry):
        _row_copy(y_ref, pos_smem[r], y0_scr, r, sem).start()
        _row_copy(y_ref, pos_smem[rows + r], y1_scr, r, sem).start()
        return carry

    lax.fori_loop(0, rows, issue, 0, unroll=ISSUE_UNROLL)
    _wait_rows(y_ref, y0_scr, sem)
    _wait_rows(y_ref, y1_scr, sem)
    gates = gates_ref[...]
    y = gates[:, 0:1] * y0_scr[...] + gates[:, 1:2] * y1_scr[...]
    o_ref[...] = _rms(x1_ref[...] + mod_ref[0, 5:6] * y) * gfin_ref[...]


def _combine(pos, y, x1, gates, mod, gfin, tokens_per_batch):
    n, d = x1.shape
    n_tiles, two_rows = pos.shape
    rows = two_rows // 2
    tok = pl.BlockSpec((rows, d), lambda i: (i, 0))
    return pl.pallas_call(
        _combine_kernel,
        grid=(n_tiles,),
        in_specs=[pl.BlockSpec(memory_space=pltpu.VMEM), pl.BlockSpec(memory_space=pl.ANY), tok,
                  pl.BlockSpec((rows, LANES), lambda i: (i, 0)),
                  pl.BlockSpec((1, 6, d), lambda i: (i * rows // tokens_per_batch, 0, 0)),
                  pl.BlockSpec((1, d), lambda i: (0, 0))],
        out_specs=tok,
        out_shape=jax.ShapeDtypeStruct((n, d), F32),
        scratch_shapes=[pltpu.SMEM((two_rows,), jnp.int32), pltpu.VMEM((rows, d), F32),
                        pltpu.VMEM((rows, d), F32), pltpu.SemaphoreType.DMA(()),
                        pltpu.SemaphoreType.DMA(())],
        compiler_params=_cparams(("arbitrary",)),
        name="moe_combine",
    )(pos, y, x1, gates, mod, gfin)


def _dispatch_plan(ids, tm, rows):
    n = ids.shape[0]
    flat = ids.reshape(-1)
    onehot = (flat[None, :] == jnp.arange(N_EXPERTS)[:, None]).astype(jnp.int32)
    csum = jnp.cumsum(onehot, axis=1)
    rank = jnp.sum(csum * onehot, axis=0) - 1
    counts = csum[:, -1]
    padded = (counts + tm - 1) // tm * tm
    ends = jnp.cumsum(padded)
    pos = (ends - padded)[flat] + rank
    n_sorted = 2 * n + N_EXPERTS * tm
    starts = jnp.arange(n_sorted // tm, dtype=jnp.int32) * tm
    tile_expert = jnp.sum((starts[:, None] >= ends[None, :]).astype(jnp.int32), axis=1)
    tile_valid = (starts < ends[-1]).astype(jnp.int32)
    last_expert = tile_expert[jnp.maximum(ends[-1] // tm - 1, 0)]
    tile_expert = jnp.where(tile_valid > 0, tile_expert, last_expert)
    order = jnp.argsort(flat, stable=True).astype(jnp.int32)
    first = jnp.cumsum(counts) - counts
    row = jnp.arange(n_sorted, dtype=jnp.int32)
    row_expert = jnp.repeat(tile_expert, tm)
    r = row - (ends - padded)[row_expert]
    live = (r < counts[row_expert]) & (jnp.repeat(tile_valid, tm) > 0)
    src = jnp.where(live, order[jnp.clip(first[row_expert] + r, 0, 2 * n - 1)] // 2, 0)
    pos = pos.reshape(n // rows, rows, 2).transpose(0, 2, 1).reshape(n // rows, 2 * rows)
    return pos.astype(jnp.int32), src.reshape(n_sorted // tm, tm), tile_expert, tile_valid


def _ffn_moe(x, oa, ob, oc, mod, lw, gfin):
    b, t, d = x.shape
    n = b * t
    assert t % ROW_TILE == 0
    x1, hf, ids, gates = _route(x, oa, ob, oc, mod, lw, 512)
    pos, src, tile_expert, tile_valid = _dispatch_plan(ids.reshape(n, LANES)[:, :2], MOE_TM, ROW_TILE)
    y = _experts(tile_expert, tile_valid, src, hf.reshape(n, d), lw, MOE_TF)
    out = _combine(pos, y, x1.reshape(n, d), gates.reshape(n, LANES), mod, gfin, t)
    return out.reshape(b, t, d)


def _rope_tables(n_lat):
    rows = n_lat // GRID_W
    row = jnp.repeat(jnp.arange(rows, dtype=F32), GRID_W)
    col = jnp.tile(jnp.arange(GRID_W, dtype=F32), rows)

    def table(rot_dim):
        nf = rot_dim // 4
        inv = ROPE_THETA ** (-jnp.arange(nf, dtype=F32) / nf)
        ang_r, ang_c = row[:, None] * inv, col[:, None] * inv
        ang = jnp.concatenate([ang_r, ang_r, ang_c, ang_c], axis=-1)
        sign = jnp.tile(jnp.concatenate([-jnp.ones(nf, F32), jnp.ones(nf, F32)]), 2)
        return jnp.cos(ang), jnp.sin(ang) * sign

    cos_h, sin_h = table(HEAD_DIM)
    cos_r, sin_r = table(B_ROPE)
    one = jnp.ones((n_lat, B_NOPE), F32)
    pad = LANES - B_NOPE - B_ROPE
    cos_b = jnp.concatenate([one, cos_r, jnp.ones((n_lat, pad), F32)], axis=-1)
    sin_b = jnp.concatenate([0 * one, sin_r, jnp.zeros((n_lat, pad), F32)], axis=-1)
    return (jnp.tile(cos_h, (1, 2)), jnp.tile(sin_h, (1, 2)), cos_b, sin_b)


def _permute_heads(w, axis):
    shape = w.shape
    w = w.reshape(shape[:axis] + (6, HEAD_DIM) + shape[axis + 1:])
    w = jnp.take(w, jnp.array(HEAD_PERM), axis=axis)
    return w.reshape(shape)


def _layer_weights(i, w_in, norm_ffn, a_q_norm, a_k_norm, b_q_norm, b_kv_norm, w_uq, w_ukv, w_out):
    d = w_in.shape[1]
    wi = w_in[i]
    qk = A_HEADS * HEAD_DIM
    kv = A_KV * HEAD_DIM
    b0 = A_COLS
    c0 = A_COLS + B_COLS
    k_rope = jnp.pad(wi[:, b0 + B_Q_RANK + B_KV_RANK:c0], ((0, 0), (B_NOPE, LANES - B_NOPE - B_ROPE)))
    w_in_p = jnp.concatenate([
        _permute_heads(wi[:, :qk], 1), wi[:, qk:A_COLS],
        wi[:, b0:b0 + B_Q_RANK + B_KV_RANK], k_rope,
        _permute_heads(wi[:, c0:c0 + qk], 1), wi[:, c0 + qk:]], axis=1).astype(BF16)
    w_uq_p = jnp.pad(w_uq[i].reshape(B_Q_RANK, B_HEADS, B_NOPE + B_ROPE),
                     ((0, 0), (0, 0), (0, LANES - B_NOPE - B_ROPE))).reshape(B_Q_RANK, B_HEADS * LANES)
    ukv = w_ukv[i].reshape(B_KV_RANK, B_HEADS, B_NOPE + B_V)
    k_up = jnp.pad(ukv[:, :, :B_NOPE], ((0, 0), (0, 0), (0, LANES - B_NOPE))).reshape(B_KV_RANK, -1)
    v_up = ukv[:, :, B_NOPE:].reshape(B_KV_RANK, -1)
    wo = w_out[i]
    w_out_p = jnp.concatenate([_permute_heads(wo[:qk], 0), wo[qk:qk + B_HEADS * B_V],
                               _permute_heads(wo[qk + B_HEADS * B_V:], 0)], axis=0).astype(BF16)
    blk = (jnp.arange(LANES)[:, None] // HEAD_DIM) == (jnp.arange(LANES)[None, :] // HEAD_DIM)
    del kv, d
    return {
        "w_in": w_in_p,
        "m2": (blk.astype(F32) / HEAD_DIM).astype(BF16),
        "a_q_gain": jnp.tile(a_q_norm[i], 2)[None], "a_k_gain": jnp.tile(a_k_norm[i], 2)[None],
        "b_q_gain": b_q_norm[i][None], "b_kv_gain": b_kv_norm[i][None],
        "w_uq": w_uq_p.astype(BF16),
        "w_ukv": jnp.concatenate([k_up, v_up], axis=1).astype(BF16),
        "w_out": w_out_p,
        "g_ffn": norm_ffn[i][None],
    }


def kernel(x, c, ctx, c_ctx, w_mod, b_mod, norm_mix, norm_ffn, w_in, a_q_norm, a_k_norm, b_q_norm,
           b_kv_norm, w_uq, w_ukv, c_sinks, w_out, w1_dense, w3_dense, w2_dense, w_router, w1_moe,
           w3_moe, w2_moe, final_norm):
    bsz, n_lat, d = x.shape
    n_ctx = ctx.shape[1]
    depth = w_mod.shape[0]
    mod_rows = 16
    cc = jnp.concatenate([c, c_ctx[None], jnp.zeros((mod_rows - bsz - 1, d), F32)], axis=0)
    mods = _modulation(cc, w_mod, b_mod)
    tables = _rope_tables(n_lat)
    no_sink = jnp.zeros((A_HEADS,), F32)
    xc = ctx
    for i in range(depth):
        last = i == depth - 1
        mod_lat = mods[i, :bsz].reshape(bsz, 6, d)
        mod_ctx = mods[i, bsz:bsz + 1].reshape(1, 6, d)
        lw = _layer_weights(i, w_in, norm_ffn, a_q_norm, a_k_norm, b_q_norm, b_kv_norm, w_uq, w_ukv,
                            w_out)
        g_mix = norm_mix[i][None]
        lat = _in_projection(x, mod_lat, True, g_mix, lw, tables, True, 512)
        cx = _in_projection(xc, mod_ctx, False, g_mix, lw, tables, False, n_ctx)
        qa, ka, va, qb, kb, vb, qc, kc, vc = lat
        cqa, cka, cva, cqb, ckb, cvb, cqc, ckc, cvc = cx

        def both(u, w):
            return jnp.concatenate([u, w], axis=2)

        sinks = jnp.take(c_sinks[i], jnp.array(HEAD_PERM))
        oa = _flash(qa, both(ka, cka), both(va, cva), no_sink, (0,) * 6, (0,) * 6, False, 512)
        ob = _flash(qb, both(kb, ckb), both(vb, cvb), no_sink, (0, 1, 2, 3), (0, 0, 1, 1), False, 512)
        oc = _window_attention(qc, both(kc, ckc), both(vc, cvc), sinks, n_lat)
        if not last:
            coa = _flash(cqa, cka, cva, no_sink, (0,) * 6, (0,) * 6, False, n_ctx)
            cob = _flash(cqb, ckb, cvb, no_sink, (0, 1, 2, 3), (0, 0, 1, 1), False, n_ctx)
            coc = _flash(cqc, ckc, cvc, sinks, (0,) * 6, (0,) * 6, True, n_ctx)
        j = i // 2
        if i % 2 == 0:
            lw.update(w1=w1_dense[j].astype(BF16), w3=w3_dense[j].astype(BF16),
                      w2=w2_dense[j].astype(BF16))
            x_new = _ffn_dense(x, oa, ob, oc, mod_lat, True, lw, 512)
            if not last:
                xc = _ffn_dense(xc, coa, cob, coc, mod_ctx, False, lw, n_ctx)
            x = x_new
        else:
            assert last, "the expert channel mixer is fused with the final norm"
            lw.update(w1=w1_moe[j].astype(BF16), w3=w3_moe[j].astype(BF16), w2=w2_moe[j].astype(BF16),
                      w_router=jnp.pad(w_router[j], ((0, 0), (0, LANES - N_EXPERTS))))
            x = _ffn_moe(x, oa, ob, oc, mod_lat, lw, final_norm[None])
    return x
```

```python
import functools

import jax
import jax.numpy as jnp
from jax import lax
from jax.experimental import pallas as pl
from jax.experimental.pallas import tpu as pltpu

F32 = jnp.float32
BF16 = jnp.bfloat16

LANES = 128
HEAD_DIM = 64
GRID_W = 64
ROPE_THETA = 10000.0
EPS = 1e-6
NEG_BIG = -1e30
A_HEADS, A_KV = 6, 2
B_HEADS, B_Q_RANK, B_KV_RANK, B_NOPE, B_ROPE, B_V = 4, 256, 128, 64, 32, 64
C_HEADS, C_KV, C_WINDOW = 6, 2, 128
N_EXPERTS = 8
A_COLS = (A_HEADS + 2 * A_KV) * HEAD_DIM
B_COLS = B_Q_RANK + B_KV_RANK + B_ROPE
VMEM_LIMIT = 56 * 1024 * 1024
LOG2E = 1.4426950408889634
FLASH_TK = 2304
MXU_COLS = 256
ISSUE_UNROLL = 8
MOE_TM = 1024
MOE_TF = 896
ROW_TILE = 1024

HEAD_PERM = tuple((j % 2) * 3 + j // 2 for j in range(6))


def _cparams(sem):
    return pltpu.CompilerParams(dimension_semantics=sem, vmem_limit_bytes=VMEM_LIMIT)


def _rms(x):
    return x * lax.rsqrt(jnp.mean(x * x, axis=-1, keepdims=True) + EPS)


def _dot(a, b):
    return jnp.dot(a, b, preferred_element_type=F32)


def _dot_nt(a, b):
    return lax.dot_general(a, b, (((1,), (1,)), ((), ())), preferred_element_type=F32)


def _split(a):
    hi = a.astype(BF16)
    return hi, (a - hi.astype(F32)).astype(BF16)


def _dot3(a, b):
    ah, al = _split(a)
    bh, bl = _split(b)
    return _dot(ah, bh) + (_dot(ah, bl) + _dot(al, bh))


def _silu(a):
    return a * jax.nn.sigmoid(a)


def _lane(shape):
    return lax.broadcasted_iota(jnp.int32, shape, len(shape) - 1)


def _mod_kernel(c_ref, w_ref, b_ref, o_ref):
    o_ref[...] = _dot3(_silu(c_ref[...]), w_ref[...]) + b_ref[...]


def _modulation(cc, w_mod, b_mod):
    depth, d, n = w_mod.shape
    rows = cc.shape[0]
    tn = 1536
    return pl.pallas_call(
        _mod_kernel,
        grid=(depth, n // tn),
        in_specs=[pl.BlockSpec((rows, d), lambda i, j: (0, 0)),
                  pl.BlockSpec((None, d, tn), lambda i, j: (i, 0, j)),
                  pl.BlockSpec((None, 1, tn), lambda i, j: (i, 0, j))],
        out_specs=pl.BlockSpec((None, rows, tn), lambda i, j: (i, 0, j)),
        out_shape=jax.ShapeDtypeStruct((depth, rows, n), F32),
        compiler_params=_cparams(("arbitrary", "arbitrary")),
        name="modulation",
    )(cc, w_mod, b_mod.reshape(depth, 1, n))


def _rope(t, cos, sin_signed, half):
    first = (_lane(t.shape) & (2 * half - 1)) < half
    rot = jnp.where(first, pltpu.roll(t, LANES - half, 1), pltpu.roll(t, half, 1))
    return t * cos + rot * sin_signed


def _store_pair(ref, p, t):
    left = _lane(t.shape) < HEAD_DIM
    ref[0, 2 * p] = jnp.where(left, t, 0.0).astype(BF16)
    ref[0, 2 * p + 1] = jnp.where(left, 0.0, t).astype(BF16)


def _proj_kernel(x_ref, mod_ref, g_ref, win_ref, m2_ref, gaq_ref, gak_ref, gbq_ref, gbkv_ref,
                 wuq_ref, wukv_ref, cosa_ref, sina_ref, cosb_ref, sinb_ref,
                 qa_ref, ka_ref, va_ref, qb_ref, kb_ref, vb_ref, qc_ref, kc_ref, vc_ref,
                 *, use_rope):
    mod = mod_ref[0]
    h = _rms(x_ref[0]) * g_ref[...]
    h = h * (1.0 + mod[1:2]) + mod[0:1]
    proj = _dot(h.astype(BF16), win_ref[...])

    def chunk(i):
        return proj[:, i * LANES:(i + 1) * LANES]

    def head_norm(t, gain):
        ms = _dot((t * t).astype(BF16), m2_ref[...])
        return t * lax.rsqrt(ms + EPS) * gain

    def rope_a(t):
        return _rope(t, cosa_ref[...], sina_ref[...], HEAD_DIM // 4) if use_rope else t

    def rope_b(t):
        return _rope(t, cosb_ref[...], sinb_ref[...], B_ROPE // 4) if use_rope else t

    scale = HEAD_DIM ** -0.5 * LOG2E
    for p in range(3):
        _store_pair(qa_ref, p, rope_a(head_norm(chunk(p), gaq_ref[...])) * scale)
    ka_ref[0, 0] = rope_a(head_norm(chunk(3), gak_ref[...])).astype(BF16)
    va_ref[0, 0] = chunk(4).astype(BF16)
    cq = _rms(proj[:, 5 * LANES:7 * LANES]) * gbq_ref[...]
    qb = _dot(cq.astype(BF16), wuq_ref[...])
    ckv = _rms(chunk(7)) * gbkv_ref[...]
    kvb = _dot(ckv.astype(BF16), wukv_ref[...])
    k_rope = rope_b(chunk(8))
    scale_b = (B_NOPE + B_ROPE) ** -0.5 * LOG2E
    for hd in range(B_HEADS):
        qb_ref[0, hd] = (rope_b(qb[:, hd * LANES:(hd + 1) * LANES]) * scale_b).astype(BF16)
        kb_ref[0, hd] = (kvb[:, hd * LANES:(hd + 1) * LANES] + k_rope).astype(BF16)
    for p in range(2):
        vb_ref[0, p] = kvb[:, (B_HEADS + p) * LANES:(B_HEADS + p + 1) * LANES].astype(BF16)
    for p in range(3):
        _store_pair(qc_ref, p, rope_a(chunk(9 + p)) * scale)
    kc_ref[0, 0] = rope_a(chunk(12)).astype(BF16)
    vc_ref[0, 0] = chunk(13).astype(BF16)


def _in_projection(x, mod, per_batch_mod, g, lw, tables, use_rope, tm):
    b, t, d = x.shape
    ncol = lw["w_in"].shape[1]
    heads = (6, 1, 1, 4, 4, 2, 6, 1, 1)

    def const(shape):
        return pl.BlockSpec(shape, lambda i, j: (0,) * len(shape))

    mod_map = (lambda i, j: (i, 0, 0)) if per_batch_mod else (lambda i, j: (0, 0, 0))
    tab = pl.BlockSpec((tm, LANES), lambda i, j: (j, 0))
    return pl.pallas_call(
        functools.partial(_proj_kernel, use_rope=use_rope),
        grid=(b, t // tm),
        in_specs=[pl.BlockSpec((1, tm, d), lambda i, j: (i, j, 0)),
                  pl.BlockSpec((1, 6, d), mod_map),
                  const((1, d)), const((d, ncol)), const((LANES, LANES)),
                  const((1, LANES)), const((1, LANES)), const((1, B_Q_RANK)), const((1, B_KV_RANK)),
                  const((B_Q_RANK, 4 * LANES)), const((B_KV_RANK, 6 * LANES)),
                  tab, tab, tab, tab],
        out_specs=[pl.BlockSpec((1, n, tm, LANES), lambda i, j: (i, 0, j, 0)) for n in heads],
        out_shape=[jax.ShapeDtypeStruct((b, n, t, LANES), BF16) for n in heads],
        compiler_params=_cparams(("parallel", "parallel")),
        name="in_projection",
    )(x, mod, g, lw["w_in"], lw["m2"], lw["a_q_gain"], lw["a_k_gain"], lw["b_q_gain"],
      lw["b_kv_gain"], lw["w_uq"], lw["w_ukv"], *tables)


def _key_chunks(t_keys, max_chunk):
    tiles = -(-t_keys // MXU_COLS)
    n = -(-tiles * MXU_COLS // max_chunk)
    sizes = [(tiles // n + (c < tiles % n)) * MXU_COLS for c in range(n)]
    sizes[-1] -= tiles * MXU_COLS - t_keys
    return tuple(sizes)


def _flash_kernel(sink_ref, q_ref, k_ref, v_ref, o_ref, *, kidx, vidx, tk, use_sink):
    nh, tq = q_ref.shape[1], q_ref.shape[2]
    t_keys = k_ref.shape[2]
    chunks = _key_chunks(t_keys, tk)
    left = _lane((tq, LANES)) < HEAD_DIM
    for pair in range(nh // 2):
        heads = (2 * pair, 2 * pair + 1)

        def step(start, size, carry, heads=heads):
            new = []
            for j, (m, l, acc) in zip(heads, carry):
                k = k_ref[0, kidx[j], pl.ds(start, size), :]
                v = v_ref[0, vidx[j], pl.ds(start, size), :]
                s = _dot_nt(q_ref[0, j], k)
                m_new = jnp.maximum(m, jnp.max(s, axis=-1, keepdims=True))
                alpha = jnp.exp2(m - m_new)
                p = jnp.exp2(s - m_new)
                l = alpha * l + jnp.sum(p, axis=-1, keepdims=True)
                acc = alpha * acc + _dot(p.astype(BF16), v)
                new.append((m_new, l, acc))
            return tuple(new)

        def init(j):
            if use_sink:
                m, l = jnp.full((tq, 1), sink_ref[j] * LOG2E, F32), jnp.ones((tq, 1), F32)
            else:
                m, l = jnp.full((tq, 1), NEG_BIG, F32), jnp.zeros((tq, 1), F32)
            return m, l, jnp.zeros((tq, LANES), F32)

        carry = tuple(init(j) for j in heads)
        start = 0
        for size in chunks:
            carry = step(start, size, carry)
            start += size
        even, odd = (c[2] / c[1] for c in carry)
        o_ref[0, pair] = jnp.where(left, even, odd).astype(BF16)


def _flash(q, k, v, sinks, kidx, vidx, use_sink, tq):
    b, nh, t, _ = q.shape
    nk, tkeys = k.shape[1], k.shape[2]
    nv = v.shape[1]
    return pl.pallas_call(
        functools.partial(_flash_kernel, kidx=kidx, vidx=vidx, tk=FLASH_TK, use_sink=use_sink),
        grid=(b, t // tq),
        in_specs=[pl.BlockSpec(memory_space=pltpu.SMEM),
                  pl.BlockSpec((1, nh, tq, LANES), lambda i, j: (i, 0, j, 0)),
                  pl.BlockSpec((1, nk, tkeys, LANES), lambda i, j: (i, 0, 0, 0)),
                  pl.BlockSpec((1, nv, tkeys, LANES), lambda i, j: (i, 0, 0, 0))],
        out_specs=pl.BlockSpec((1, nh // 2, tq, LANES), lambda i, j: (i, 0, j, 0)),
        out_shape=jax.ShapeDtypeStruct((b, nh // 2, t, LANES), BF16),
        compiler_params=_cparams(("parallel", "parallel")),
        name="flash_attention",
    )(sinks, q, k, v)


def _window_block(sink, q, ks, vs, has_prev, has_next, w):
    k = jnp.concatenate(ks, axis=0)
    v = jnp.concatenate(vs, axis=0)
    s = _dot_nt(q, k)
    qq = lax.broadcasted_iota(jnp.int32, s.shape, 0) & (w - 1)
    col = lax.broadcasted_iota(jnp.int32, s.shape, 1)
    never = 4 * w
    off_prev = jnp.where(has_prev, 0, never)
    off_next = jnp.where(has_next, 0, never)
    valid = ((col < w) & (col >= qq + off_prev)) | ((col >= w) & (col < 2 * w)) \
        | ((col >= 2 * w) & (col < 3 * w) & (col - 2 * w + off_next <= qq)) | (col >= 3 * w)
    s = jnp.where(valid, s, NEG_BIG)
    m = jnp.maximum(sink, jnp.max(s, axis=-1, keepdims=True))
    e = jnp.exp2(s - m)
    denom = jnp.exp2(sink - m) + jnp.sum(e, axis=-1, keepdims=True)
    return _dot(e.astype(BF16), v) / denom


def _window_kernel(sink_ref, q_ref, kp_ref, kc_ref, kn_ref, kx_ref, vp_ref, vc_ref, vn_ref, vx_ref,
                   o_ref):
    nh, w = q_ref.shape[1], q_ref.shape[2] // 2
    i, steps = pl.program_id(1), pl.num_programs(1)
    sink = jnp.concatenate([jnp.full((w, 1), sink_ref[j] * LOG2E, F32) for j in range(nh)], axis=0)
    k_lo, k_hi = kc_ref[0, 0, :w], kc_ref[0, 0, w:]
    v_lo, v_hi = vc_ref[0, 0, :w], vc_ref[0, 0, w:]
    kx, vx = kx_ref[0, 0], vx_ref[0, 0]
    q_first = q_ref[0, :, :w].reshape(nh * w, LANES)
    q_second = q_ref[0, :, w:].reshape(nh * w, LANES)
    first = _window_block(sink, q_first, [kp_ref[0, 0], k_lo, k_hi, kx],
                          [vp_ref[0, 0], v_lo, v_hi, vx], i >= 1, True, w)
    second = _window_block(sink, q_second, [k_lo, k_hi, kn_ref[0, 0], kx],
                           [v_lo, v_hi, vn_ref[0, 0], vx], True, i <= steps - 2, w)
    left = _lane((w, LANES)) < HEAD_DIM
    for p in range(nh // 2):
        for half, o in enumerate((first, second)):
            o_ref[0, p, half * w:(half + 1) * w] = jnp.where(
                left, o[2 * p * w:(2 * p + 1) * w], o[(2 * p + 1) * w:(2 * p + 2) * w]).astype(BF16)


def _window_attention(q, k_all, v_all, sinks, n_lat):
    b, nh, _, _ = q.shape
    w = C_WINDOW
    nb = n_lat // w
    n_ctx = k_all.shape[2] - n_lat
    assert n_lat % n_ctx == 0 and w & (w - 1) == 0 and nb % 2 == 0 and n_ctx % (2 * w) == 0
    prev = pl.BlockSpec((1, 1, w, LANES), lambda i, j: (i, 0, jnp.maximum(2 * j - 1, 0), 0))
    cur = pl.BlockSpec((1, 1, 2 * w, LANES), lambda i, j: (i, 0, j, 0))
    nxt = pl.BlockSpec((1, 1, w, LANES), lambda i, j: (i, 0, jnp.minimum(2 * j + 2, nb - 1), 0))
    ctx = pl.BlockSpec((1, 1, n_ctx, LANES), lambda i, j: (i, 0, n_lat // n_ctx, 0))
    return pl.pallas_call(
        _window_kernel,
        grid=(b, nb // 2),
        in_specs=[pl.BlockSpec(memory_space=pltpu.SMEM),
                  pl.BlockSpec((1, nh, 2 * w, LANES), lambda i, j: (i, 0, j, 0)),
                  prev, cur, nxt, ctx, prev, cur, nxt, ctx],
        out_specs=pl.BlockSpec((1, nh // 2, 2 * w, LANES), lambda i, j: (i, 0, j, 0)),
        out_shape=jax.ShapeDtypeStruct((b, nh // 2, n_lat, LANES), BF16),
        compiler_params=_cparams(("parallel", "parallel")),
        name="window_attention",
    )(sinks, q, k_all, k_all, k_all, k_all, v_all, v_all, v_all, v_all)


def _mixed_residual(x_ref, oa_ref, ob_ref, oc_ref, mod, wout_ref):
    att = jnp.concatenate([oa_ref[0, 0], oa_ref[0, 1], oa_ref[0, 2], ob_ref[0, 0], ob_ref[0, 1],
                           oc_ref[0, 0], oc_ref[0, 1], oc_ref[0, 2]], axis=-1)
    return x_ref[0] + mod[2:3] * _dot(att, wout_ref[...])


def _ffn_dense_kernel(x_ref, oa_ref, ob_ref, oc_ref, mod_ref, wout_ref, g_ref, w1_ref, w3_ref, w2_ref,
                      o_ref, *, n_chunks):
    mod = mod_ref[0]
    x1 = _mixed_residual(x_ref, oa_ref, ob_ref, oc_ref, mod, wout_ref)
    hf = ((_rms(x1) * g_ref[...]) * (1.0 + mod[4:5]) + mod[3:4]).astype(BF16)
    fc = w1_ref.shape[1] // n_chunks
    y = jnp.zeros(x1.shape, F32)
    for c in range(n_chunks):
        a = _dot(hf, w1_ref[:, c * fc:(c + 1) * fc])
        gate = _dot(hf, w3_ref[:, c * fc:(c + 1) * fc])
        y = y + _dot((_silu(a) * gate).astype(BF16), w2_ref[c * fc:(c + 1) * fc, :])
    o_ref[0] = x1 + mod[5:6] * y


def _resident(shape):
    return pl.BlockSpec(shape, lambda i, j: (0,) * len(shape), pipeline_mode=pl.Buffered(1))


def _ffn_dense(x, oa, ob, oc, mod, per_batch_mod, lw, tm):
    b, t, d = x.shape
    f = lw["w1"].shape[1]
    mod_map = (lambda i, j: (i, 0, 0)) if per_batch_mod else (lambda i, j: (0, 0, 0))

    def att(n):
        return pl.BlockSpec((1, n, tm, LANES), lambda i, j: (i, 0, j, 0))

    return pl.pallas_call(
        functools.partial(_ffn_dense_kernel, n_chunks=2),
        grid=(b, t // tm),
        in_specs=[pl.BlockSpec((1, tm, d), lambda i, j: (i, j, 0)), att(3), att(2), att(3),
                  pl.BlockSpec((1, 6, d), mod_map),
                  _resident((d, d)), _resident((1, d)),
                  _resident((d, f)), _resident((d, f)), _resident((f, d))],
        out_specs=pl.BlockSpec((1, tm, d), lambda i, j: (i, j, 0)),
        out_shape=jax.ShapeDtypeStruct((b, t, d), F32),
        compiler_params=_cparams(("parallel", "parallel")),
        name="ffn_dense",
    )(x, oa, ob, oc, mod, lw["w_out"], lw["g_ffn"], lw["w1"], lw["w3"], lw["w2"])


def _route_kernel(x_ref, oa_ref, ob_ref, oc_ref, mod_ref, wout_ref, g_ref, wr_ref,
                  x1_ref, hf_ref, ids_ref, gates_ref):
    mod = mod_ref[0]
    x1 = _mixed_residual(x_ref, oa_ref, ob_ref, oc_ref, mod, wout_ref)
    hf = (_rms(x1) * g_ref[...]) * (1.0 + mod[4:5]) + mod[3:4]
    x1_ref[0] = x1
    hf_ref[0] = hf
    logits = _dot3(hf, wr_ref[...])
    lane = _lane(logits.shape)
    logits = jnp.where(lane < N_EXPERTS, logits, NEG_BIG)
    m1 = jnp.max(logits, axis=-1, keepdims=True)
    i1 = jnp.min(jnp.where(logits == m1, lane, LANES), axis=-1, keepdims=True)
    rest = jnp.where(lane == i1, NEG_BIG, logits)
    m2 = jnp.max(rest, axis=-1, keepdims=True)
    i2 = jnp.min(jnp.where(rest == m2, lane, LANES), axis=-1, keepdims=True)
    e2 = jnp.exp(m2 - m1)
    g1 = 1.0 / (1.0 + e2)
    ids_ref[0] = jnp.where(lane == 0, i1, jnp.where(lane == 1, i2, 0))
    gates_ref[0] = jnp.where(lane == 0, g1, jnp.where(lane == 1, e2 * g1, 0.0))


def _route(x, oa, ob, oc, mod, lw, tm):
    b, t, d = x.shape

    def att(n):
        return pl.BlockSpec((1, n, tm, LANES), lambda i, j: (i, 0, j, 0))

    def const(shape):
        return pl.BlockSpec(shape, lambda i, j: (0,) * len(shape))

    tok = pl.BlockSpec((1, tm, d), lambda i, j: (i, j, 0))
    meta = pl.BlockSpec((1, tm, LANES), lambda i, j: (i, j, 0))
    return pl.pallas_call(
        _route_kernel,
        grid=(b, t // tm),
        in_specs=[tok, att(3), att(2), att(3), pl.BlockSpec((1, 6, d), lambda i, j: (i, 0, 0)),
                  const((d, d)), const((1, d)), const((d, LANES))],
        out_specs=[tok, tok, meta, meta],
        out_shape=[jax.ShapeDtypeStruct((b, t, d), F32), jax.ShapeDtypeStruct((b, t, d), F32),
                   jax.ShapeDtypeStruct((b, t, LANES), jnp.int32),
                   jax.ShapeDtypeStruct((b, t, LANES), F32)],
        compiler_params=_cparams(("parallel", "parallel")),
        name="moe_route",
    )(x, oa, ob, oc, mod, lw["w_out"], lw["g_ffn"], lw["w_router"])


def _row_copy(src_ref, src_row, dst_ref, dst_row, sem):
    return pltpu.make_async_copy(src_ref.at[pl.ds(src_row, 1)], dst_ref.at[pl.ds(dst_row, 1)], sem)


def _load_indices(idx_ref, tile, idx_smem, sem):
    cp = pltpu.make_async_copy(idx_ref.at[tile], idx_smem, sem)
    cp.start()
    cp.wait()


def _wait_rows(src_ref, dst_ref, sem):
    pltpu.make_async_copy(src_ref.at[pl.ds(0, dst_ref.shape[0])], dst_ref, sem).wait()


def _experts_kernel(te_ref, tv_ref, src_ref, hf_ref, w1_ref, w3_ref, w2_ref, y_ref,
                    src_smem, xg_scr, xb_scr, acc_scr, src_sem, sems):
    i, f = pl.program_id(0), pl.program_id(1)
    tm = xb_scr.shape[0]

    def gather(tile):
        slot = tile % 2
        _load_indices(src_ref, tile, src_smem, src_sem)

        def issue(r, carry):
            _row_copy(hf_ref, src_smem[r], xg_scr.at[slot], r, sems.at[slot]).start()
            return carry

        lax.fori_loop(0, tm, issue, 0, unroll=ISSUE_UNROLL)

    @pl.when((f == 0) & (i == 0))
    def _():
        gather(0)

    @pl.when((f == 0) & (tv_ref[i] > 0))
    def _():
        slot = i % 2
        _wait_rows(hf_ref, xg_scr.at[slot], sems.at[slot])
        xb_scr[...] = xg_scr[slot].astype(BF16)

    @pl.when(f == 0)
    def _():
        acc_scr[...] = jnp.zeros(acc_scr.shape, F32)
        nxt = jnp.minimum(i + 1, pl.num_programs(0) - 1)

        @pl.when((i + 1 < pl.num_programs(0)) & (tv_ref[nxt] > 0))
        def _():
            gather(i + 1)

    @pl.when(tv_ref[i] > 0)
    def _():
        xb = xb_scr[...]
        a = _dot(xb, w1_ref[...])
        gate = _dot(xb, w3_ref[...])
        acc_scr[...] += _dot((_silu(a) * gate).astype(BF16), w2_ref[...])

    @pl.when(f == pl.num_programs(1) - 1)
    def _():
        y_ref[...] = acc_scr[...]


def _experts(tile_expert, tile_valid, src, hf, lw, tf):
    n_tiles, tm = src.shape
    d = hf.shape[1]
    f = lw["w1"].shape[2]
    grid_spec = pltpu.PrefetchScalarGridSpec(
        num_scalar_prefetch=2,
        grid=(n_tiles, f // tf),
        in_specs=[pl.BlockSpec(memory_space=pltpu.VMEM), pl.BlockSpec(memory_space=pl.ANY),
                  pl.BlockSpec((None, d, tf), lambda i, k, te, tv: (te[i], 0, k)),
                  pl.BlockSpec((None, d, tf), lambda i, k, te, tv: (te[i], 0, k)),
                  pl.BlockSpec((None, tf, d), lambda i, k, te, tv: (te[i], k, 0))],
        out_specs=pl.BlockSpec((tm, d), lambda i, k, te, tv: (i, 0)),
        scratch_shapes=[pltpu.SMEM((tm,), jnp.int32), pltpu.VMEM((2, tm, d), F32),
                        pltpu.VMEM((tm, d), BF16), pltpu.VMEM((tm, d), F32),
                        pltpu.SemaphoreType.DMA(()), pltpu.SemaphoreType.DMA((2,))])
    return pl.pallas_call(
        _experts_kernel,
        grid_spec=grid_spec,
        out_shape=jax.ShapeDtypeStruct((n_tiles * tm, d), F32),
        compiler_params=_cparams(("arbitrary", "arbitrary")),
        name="moe_experts",
    )(tile_expert, tile_valid, src, hf, lw["w1"], lw["w3"], lw["w2"])


def _combine_kernel(pos_ref, y_ref, x1_ref, gates_ref, mod_ref, gfin_ref, o_ref,
                    pos_smem, y0_scr, y1_scr, pos_sem, sem):
    rows = y0_scr.shape[0]
    _load_indices(pos_ref, pl.program_id(0), pos_smem, pos_sem)

    def issue(r, carry):
        _row_copy(y_ref, pos_smem[r], y0_scr, r, sem).start()
        _row_copy(y_ref, pos_smem[rows + r], y1_scr, r, sem).start()
        return carry

    lax.fori_loop(0, rows, issue, 0, unroll=ISSUE_UNROLL)
    _wait_rows(y_ref, y0_scr, sem)
    _wait_rows(y_ref, y1_scr, sem)
    gates = gates_ref[...]
    y = gates[:, 0:1] * y0_scr[...] + gates[:, 1:2] * y1_scr[...]
    o_ref[...] = _rms(x1_ref[...] + mod_ref[0, 5:6] * y) * gfin_ref[...]


def _combine(pos, y, x1, gates, mod, gfin, tokens_per_batch):
    n, d = x1.shape
    n_tiles, two_rows = pos.shape
    rows = two_rows // 2
    tok = pl.BlockSpec((rows, d), lambda i: (i, 0))
    return pl.pallas_call(
        _combine_kernel,
        grid=(n_tiles,),
        in_specs=[pl.BlockSpec(memory_space=pltpu.VMEM), pl.BlockSpec(memory_space=pl.ANY), tok,
                  pl.BlockSpec((rows, LANES), lambda i: (i, 0)),
                  pl.BlockSpec((1, 6, d), lambda i: (i * rows // tokens_per_batch, 0, 0)),
                  pl.BlockSpec((1, d), lambda i: (0, 0))],
        out_specs=tok,
        out_shape=jax.ShapeDtypeStruct((n, d), F32),
        scratch_shapes=[pltpu.SMEM((two_rows,), jnp.int32), pltpu.VMEM((rows, d), F32),
                        pltpu.VMEM((rows, d), F32), pltpu.SemaphoreType.DMA(()),
                        pltpu.SemaphoreType.DMA(())],
        compiler_params=_cparams(("arbitrary",)),
        name="moe_combine",
    )(pos, y, x1, gates, mod, gfin)


def _dispatch_plan(ids, tm, rows):
    n = ids.shape[0]
    flat = ids.reshape(-1)
    onehot = (flat[None, :] == jnp.arange(N_EXPERTS)[:, None]).astype(jnp.int32)
    csum = jnp.cumsum(onehot, axis=1)
    rank = jnp.sum(csum * onehot, axis=0) - 1
    counts = csum[:, -1]
    padded = (counts + tm - 1) // tm * tm
    ends = jnp.cumsum(padded)
    pos = (ends - padded)[flat] + rank
    n_sorted = 2 * n + N_EXPERTS * tm
    starts = jnp.arange(n_sorted // tm, dtype=jnp.int32) * tm
    tile_expert = jnp.sum((starts[:, None] >= ends[None, :]).astype(jnp.int32), axis=1)
    tile_valid = (starts < ends[-1]).astype(jnp.int32)
    last_expert = tile_expert[jnp.maximum(ends[-1] // tm - 1, 0)]
    tile_expert = jnp.where(tile_valid > 0, tile_expert, last_expert)
    order = jnp.argsort(flat, stable=True).astype(jnp.int32)
    first = jnp.cumsum(counts) - counts
    row = jnp.arange(n_sorted, dtype=jnp.int32)
    row_expert = jnp.repeat(tile_expert, tm)
    r = row - (ends - padded)[row_expert]
    live = (r < counts[row_expert]) & (jnp.repeat(tile_valid, tm) > 0)
    src = jnp.where(live, order[jnp.clip(first[row_expert] + r, 0, 2 * n - 1)] // 2, 0)
    pos = pos.reshape(n // rows, rows, 2).transpose(0, 2, 1).reshape(n // rows, 2 * rows)
    return pos.astype(jnp.int32), src.reshape(n_sorted // tm, tm), tile_expert, tile_valid


def _ffn_moe(x, oa, ob, oc, mod, lw, gfin):
    b, t, d = x.shape
    n = b * t
    assert t % ROW_TILE == 0
    x1, hf, ids, gates = _route(x, oa, ob, oc, mod, lw, 512)
    pos, src, tile_expert, tile_valid = _dispatch_plan(ids.reshape(n, LANES)[:, :2], MOE_TM, ROW_TILE)
    y = _experts(tile_expert, tile_valid, src, hf.reshape(n, d), lw, MOE_TF)
    out = _combine(pos, y, x1.reshape(n, d), gates.reshape(n, LANES), mod, gfin, t)
    return out.reshape(b, t, d)


def _rope_tables(n_lat):
    rows = n_lat // GRID_W
    row = jnp.repeat(jnp.arange(rows, dtype=F32), GRID_W)
    col = jnp.tile(jnp.arange(GRID_W, dtype=F32), rows)

    def table(rot_dim):
        nf = rot_dim // 4
        inv = ROPE_THETA ** (-jnp.arange(nf, dtype=F32) / nf)
        ang_r, ang_c = row[:, None] * inv, col[:, None] * inv
        ang = jnp.concatenate([ang_r, ang_r, ang_c, ang_c], axis=-1)
        sign = jnp.tile(jnp.concatenate([-jnp.ones(nf, F32), jnp.ones(nf, F32)]), 2)
        return jnp.cos(ang), jnp.sin(ang) * sign

    cos_h, sin_h = table(HEAD_DIM)
    cos_r, sin_r = table(B_ROPE)
    one = jnp.ones((n_lat, B_NOPE), F32)
    pad = LANES - B_NOPE - B_ROPE
    cos_b = jnp.concatenate([one, cos_r, jnp.ones((n_lat, pad), F32)], axis=-1)
    sin_b = jnp.concatenate([0 * one, sin_r, jnp.zeros((n_lat, pad), F32)], axis=-1)
    return (jnp.tile(cos_h, (1, 2)), jnp.tile(sin_h, (1, 2)), cos_b, sin_b)


def _permute_heads(w, axis):
    shape = w.shape
    w = w.reshape(shape[:axis] + (6, HEAD_DIM) + shape[axis + 1:])
    w = jnp.take(w, jnp.array(HEAD_PERM), axis=axis)
    return w.reshape(shape)


def _layer_weights(i, w_in, norm_ffn, a_q_norm, a_k_norm, b_q_norm, b_kv_norm, w_uq, w_ukv, w_out):
    d = w_in.shape[1]
    wi = w_in[i]
    qk = A_HEADS * HEAD_DIM
    kv = A_KV * HEAD_DIM
    b0 = A_COLS
    c0 = A_COLS + B_COLS
    k_rope = jnp.pad(wi[:, b0 + B_Q_RANK + B_KV_RANK:c0], ((0, 0), (B_NOPE, LANES - B_NOPE - B_ROPE)))
    w_in_p = jnp.concatenate([
        _permute_heads(wi[:, :qk], 1), wi[:, qk:A_COLS],
        wi[:, b0:b0 + B_Q_RANK + B_KV_RANK], k_rope,
        _permute_heads(wi[:, c0:c0 + qk], 1), wi[:, c0 + qk:]], axis=1).astype(BF16)
    w_uq_p = jnp.pad(w_uq[i].reshape(B_Q_RANK, B_HEADS, B_NOPE + B_ROPE),
                     ((0, 0), (0, 0), (0, LANES - B_NOPE - B_ROPE))).reshape(B_Q_RANK, B_HEADS * LANES)
    ukv = w_ukv[i].reshape(B_KV_RANK, B_HEADS, B_NOPE + B_V)
    k_up = jnp.pad(ukv[:, :, :B_NOPE], ((0, 0), (0, 0), (0, LANES - B_NOPE))).reshape(B_KV_RANK, -1)
    v_up = ukv[:, :, B_NOPE:].reshape(B_KV_RANK, -1)
    wo = w_out[i]
    w_out_p = jnp.concatenate([_permute_heads(wo[:qk], 0), wo[qk:qk + B_HEADS * B_V],
                               _permute_heads(wo[qk + B_HEADS * B_V:], 0)], axis=0).astype(BF16)
    blk = (jnp.arange(LANES)[:, None] // HEAD_DIM) == (jnp.arange(LANES)[None, :] // HEAD_DIM)
    del kv, d
    return {
        "w_in": w_in_p,
        "m2": (blk.astype(F32) / HEAD_DIM).astype(BF16),
        "a_q_gain": jnp.tile(a_q_norm[i], 2)[None], "a_k_gain": jnp.tile(a_k_norm[i], 2)[None],
        "b_q_gain": b_q_norm[i][None], "b_kv_gain": b_kv_norm[i][None],
        "w_uq": w_uq_p.astype(BF16),
        "w_ukv": jnp.concatenate([k_up, v_up], axis=1).astype(BF16),
        "w_out": w_out_p,
        "g_ffn": norm_ffn[i][None],
    }


def kernel(x, c, ctx, c_ctx, w_mod, b_mod, norm_mix, norm_ffn, w_in, a_q_norm, a_k_norm, b_q_norm,
           b_kv_norm, w_uq, w_ukv, c_sinks, w_out, w1_dense, w3_dense, w2_dense, w_router, w1_moe,
           w3_moe, w2_moe, final_norm):
    bsz, n_lat, d = x.shape
    n_ctx = ctx.shape[1]
    depth = w_mod.shape[0]
    mod_rows = 16
    cc = jnp.concatenate([c, c_ctx[None], jnp.zeros((mod_rows - bsz - 1, d), F32)], axis=0)
    mods = _modulation(cc, w_mod, b_mod)
    tables = _rope_tables(n_lat)
    no_sink = jnp.zeros((A_HEADS,), F32)
    xc = ctx
    for i in range(depth):
        last = i == depth - 1
        mod_lat = mods[i, :bsz].reshape(bsz, 6, d)
        mod_ctx = mods[i, bsz:bsz + 1].reshape(1, 6, d)
        lw = _layer_weights(i, w_in, norm_ffn, a_q_norm, a_k_norm, b_q_norm, b_kv_norm, w_uq, w_ukv,
                            w_out)
        g_mix = norm_mix[i][None]
        lat = _in_projection(x, mod_lat, True, g_mix, lw, tables, True, 512)
        cx = _in_projection(xc, mod_ctx, False, g_mix, lw, tables, False, n_ctx)
        qa, ka, va, qb, kb, vb, qc, kc, vc = lat
        cqa, cka, cva, cqb, ckb, cvb, cqc, ckc, cvc = cx

        def both(u, w):
            return jnp.concatenate([u, w], axis=2)

        sinks = jnp.take(c_sinks[i], jnp.array(HEAD_PERM))
        oa = _flash(qa, both(ka, cka), both(va, cva), no_sink, (0,) * 6, (0,) * 6, False, 512)
        ob = _flash(qb, both(kb, ckb), both(vb, cvb), no_sink, (0, 1, 2, 3), (0, 0, 1, 1), False, 512)
        oc = _window_attention(qc, both(kc, ckc), both(vc, cvc), sinks, n_lat)
        if not last:
            coa = _flash(cqa, cka, cva, no_sink, (0,) * 6, (0,) * 6, False, n_ctx)
            cob = _flash(cqb, ckb, cvb, no_sink, (0, 1, 2, 3), (0, 0, 1, 1), False, n_ctx)
            coc = _flash(cqc, ckc, cvc, sinks, (0,) * 6, (0,) * 6, True, n_ctx)
        j = i // 2
        if i % 2 == 0:
            lw.update(w1=w1_dense[j].astype(BF16), w3=w3_dense[j].astype(BF16),
                      w2=w2_dense[j].astype(BF16))
            x_new = _ffn_dense(x, oa, ob, oc, mod_lat, True, lw, 512)
            if not last:
                xc = _ffn_dense(xc, coa, cob, coc, mod_ctx, False, lw, n_ctx)
            x = x_new
        else:
            assert last, "the expert channel mixer is fused with the final norm"
            lw.update(w1=w1_moe[j].astype(BF16), w3=w3_moe[j].astype(BF16), w2=w2_moe[j].astype(BF16),
                      w_router=jnp.pad(w_router[j], ((0, 0), (0, LANES - N_EXPERTS))))
            x = _ffn_moe(x, oa, ob, oc, mod_lat, lw, final_norm[None])
    return x
```

```python
import functools

import jax
import jax.numpy as jnp
from jax import lax
from jax.experimental import pallas as pl
from jax.experimental.pallas import tpu as pltpu

F32 = jnp.float32
BF16 = jnp.bfloat16

LANES = 128
HEAD_DIM = 64
GRID_W = 64
ROPE_THETA = 10000.0
EPS = 1e-6
NEG_BIG = -1e30
A_HEADS, A_KV = 6, 2
B_HEADS, B_Q_RANK, B_KV_RANK, B_NOPE, B_ROPE, B_V = 4, 256, 128, 64, 32, 64
C_HEADS, C_KV, C_WINDOW = 6, 2, 128
N_EXPERTS = 8
A_COLS = (A_HEADS + 2 * A_KV) * HEAD_DIM
B_COLS = B_Q_RANK + B_KV_RANK + B_ROPE
VMEM_LIMIT = 56 * 1024 * 1024
LOG2E = 1.4426950408889634
FLASH_TK = 2304
MXU_COLS = 256
WINDOWS_PER_STEP = 4
ISSUE_UNROLL = 8
MOE_TM = 1024
MOE_TF = 896
ROW_TILE = 1024

HEAD_PERM = tuple((j % 2) * 3 + j // 2 for j in range(6))


def _cparams(sem):
    return pltpu.CompilerParams(dimension_semantics=sem, vmem_limit_bytes=VMEM_LIMIT)


def _rms(x):
    return x * lax.rsqrt(jnp.mean(x * x, axis=-1, keepdims=True) + EPS)


def _dot(a, b):
    return jnp.dot(a, b, preferred_element_type=F32)


def _dot_nt(a, b):
    return lax.dot_general(a, b, (((1,), (1,)), ((), ())), preferred_element_type=F32)


def _split(a):
    hi = a.astype(BF16)
    return hi, (a - hi.astype(F32)).astype(BF16)


def _dot3(a, b):
    ah, al = _split(a)
    bh, bl = _split(b)
    return _dot(ah, bh) + (_dot(ah, bl) + _dot(al, bh))


def _silu(a):
    return a * jax.nn.sigmoid(a)


def _lane(shape):
    return lax.broadcasted_iota(jnp.int32, shape, len(shape) - 1)


def _mod_kernel(c_ref, w_ref, b_ref, o_ref):
    o_ref[...] = _dot3(_silu(c_ref[...]), w_ref[...]) + b_ref[...]


def _modulation(cc, w_mod, b_mod):
    depth, d, n = w_mod.shape
    rows = cc.shape[0]
    tn = 1536
    return pl.pallas_call(
        _mod_kernel,
        grid=(depth, n // tn),
        in_specs=[pl.BlockSpec((rows, d), lambda i, j: (0, 0)),
                  pl.BlockSpec((None, d, tn), lambda i, j: (i, 0, j)),
                  pl.BlockSpec((None, 1, tn), lambda i, j: (i, 0, j))],
        out_specs=pl.BlockSpec((None, rows, tn), lambda i, j: (i, 0, j)),
        out_shape=jax.ShapeDtypeStruct((depth, rows, n), F32),
        compiler_params=_cparams(("arbitrary", "arbitrary")),
        name="modulation",
    )(cc, w_mod, b_mod.reshape(depth, 1, n))


def _rope(t, cos, sin_signed, half):
    first = (_lane(t.shape) & (2 * half - 1)) < half
    rot = jnp.where(first, pltpu.roll(t, LANES - half, 1), pltpu.roll(t, half, 1))
    return t * cos + rot * sin_signed


def _store_pair(ref, p, rows, t):
    left = _lane(t.shape) < HEAD_DIM
    ref[0, 2 * p, rows] = jnp.where(left, t, 0.0).astype(BF16)
    ref[0, 2 * p + 1, rows] = jnp.where(left, 0.0, t).astype(BF16)


def _proj_kernel(x_ref, mod_ref, g_ref, win_ref, m2_ref, gaq_ref, gak_ref, gbq_ref, gbkv_ref,
                 wuq_ref, wukv_ref, cosa_ref, sina_ref, cosb_ref, sinb_ref,
                 qa_ref, ka_ref, va_ref, qb_ref, kb_ref, vb_ref, qc_ref, kc_ref, vc_ref,
                 *, use_rope, n_parts):
    part_rows = x_ref.shape[1] // n_parts
    for part in range(n_parts):
        _proj_rows(slice(part * part_rows, (part + 1) * part_rows), use_rope, x_ref, mod_ref, g_ref,
                   win_ref, m2_ref, gaq_ref, gak_ref, gbq_ref, gbkv_ref, wuq_ref, wukv_ref, cosa_ref,
                   sina_ref, cosb_ref, sinb_ref, qa_ref, ka_ref, va_ref, qb_ref, kb_ref, vb_ref, qc_ref,
                   kc_ref, vc_ref)


def _proj_rows(rows, use_rope, x_ref, mod_ref, g_ref, win_ref, m2_ref, gaq_ref, gak_ref, gbq_ref,
               gbkv_ref, wuq_ref, wukv_ref, cosa_ref, sina_ref, cosb_ref, sinb_ref,
               qa_ref, ka_ref, va_ref, qb_ref, kb_ref, vb_ref, qc_ref, kc_ref, vc_ref):
    mod = mod_ref[0]
    h = _rms(x_ref[0, rows]) * g_ref[...]
    h = h * (1.0 + mod[1:2]) + mod[0:1]
    proj = _dot(h.astype(BF16), win_ref[...])

    def chunk(i):
        return proj[:, i * LANES:(i + 1) * LANES]

    def head_norm(t, gain):
        ms = _dot((t * t).astype(BF16), m2_ref[...])
        return t * lax.rsqrt(ms + EPS) * gain

    def rope_a(t):
        return _rope(t, cosa_ref[rows], sina_ref[rows], HEAD_DIM // 4) if use_rope else t

    def rope_b(t):
        return _rope(t, cosb_ref[rows], sinb_ref[rows], B_ROPE // 4) if use_rope else t

    scale = HEAD_DIM ** -0.5 * LOG2E
    for p in range(3):
        _store_pair(qa_ref, p, rows, rope_a(head_norm(chunk(p), gaq_ref[...])) * scale)
    ka_ref[0, 0, rows] = rope_a(head_norm(chunk(3), gak_ref[...])).astype(BF16)
    va_ref[0, 0, rows] = chunk(4).astype(BF16)
    cq = _rms(proj[:, 5 * LANES:7 * LANES]) * gbq_ref[...]
    qb = _dot(cq.astype(BF16), wuq_ref[...])
    ckv = _rms(chunk(7)) * gbkv_ref[...]
    kvb = _dot(ckv.astype(BF16), wukv_ref[...])
    k_rope = rope_b(chunk(8))
    scale_b = (B_NOPE + B_ROPE) ** -0.5 * LOG2E
    for hd in range(B_HEADS):
        qb_ref[0, hd, rows] = (rope_b(qb[:, hd * LANES:(hd + 1) * LANES]) * scale_b).astype(BF16)
        kb_ref[0, hd, rows] = (kvb[:, hd * LANES:(hd + 1) * LANES] + k_rope).astype(BF16)
    for p in range(2):
        vb_ref[0, p, rows] = kvb[:, (B_HEADS + p) * LANES:(B_HEADS + p + 1) * LANES].astype(BF16)
    for p in range(3):
        _store_pair(qc_ref, p, rows, rope_a(chunk(9 + p)) * scale)
    kc_ref[0, 0, rows] = rope_a(chunk(12)).astype(BF16)
    vc_ref[0, 0, rows] = chunk(13).astype(BF16)


def _in_projection(x, mod, per_batch_mod, g, lw, tables, use_rope, tm):
    b, t, d = x.shape
    ncol = lw["w_in"].shape[1]
    heads = (6, 1, 1, 4, 4, 2, 6, 1, 1)

    def const(shape):
        return pl.BlockSpec(shape, lambda i, j: (0,) * len(shape))

    mod_map = (lambda i, j: (i, 0, 0)) if per_batch_mod else (lambda i, j: (0, 0, 0))
    tab = pl.BlockSpec((tm, LANES), lambda i, j: (j, 0))
    return pl.pallas_call(
        functools.partial(_proj_kernel, use_rope=use_rope, n_parts=2),
        grid=(b, t // tm),
        in_specs=[pl.BlockSpec((1, tm, d), lambda i, j: (i, j, 0)),
                  pl.BlockSpec((1, 6, d), mod_map),
                  const((1, d)), const((d, ncol)), const((LANES, LANES)),
                  const((1, LANES)), const((1, LANES)), const((1, B_Q_RANK)), const((1, B_KV_RANK)),
                  const((B_Q_RANK, 4 * LANES)), const((B_KV_RANK, 6 * LANES)),
                  tab, tab, tab, tab],
        out_specs=[pl.BlockSpec((1, n, tm, LANES), lambda i, j: (i, 0, j, 0)) for n in heads],
        out_shape=[jax.ShapeDtypeStruct((b, n, t, LANES), BF16) for n in heads],
        compiler_params=_cparams(("parallel", "parallel")),
        name="in_projection",
    )(x, mod, g, lw["w_in"], lw["m2"], lw["a_q_gain"], lw["a_k_gain"], lw["b_q_gain"],
      lw["b_kv_gain"], lw["w_uq"], lw["w_ukv"], *tables)


def _key_chunks(t_keys, max_chunk):
    tiles = -(-t_keys // MXU_COLS)
    n = -(-tiles * MXU_COLS // max_chunk)
    sizes = [(tiles // n + (c < tiles % n)) * MXU_COLS for c in range(n)]
    sizes[-1] -= tiles * MXU_COLS - t_keys
    return tuple(sizes)


def _flash_kernel(sink_ref, q_ref, k_ref, v_ref, o_ref, *, kidx, vidx, tk, use_sink):
    nh, tq = q_ref.shape[1], q_ref.shape[2]
    t_keys = k_ref.shape[2]
    chunks = _key_chunks(t_keys, tk)
    left = _lane((tq, LANES)) < HEAD_DIM
    for pair in range(nh // 2):
        heads = (2 * pair, 2 * pair + 1)

        def step(start, size, carry, heads=heads):
            new = []
            for j, (m, l, acc) in zip(heads, carry):
                k = k_ref[0, kidx[j], pl.ds(start, size), :]
                v = v_ref[0, vidx[j], pl.ds(start, size), :]
                s = _dot_nt(q_ref[0, j], k)
                m_new = jnp.maximum(m, jnp.max(s, axis=-1, keepdims=True))
                alpha = jnp.exp2(m - m_new)
                p = jnp.exp2(s - m_new)
                l = alpha * l + jnp.sum(p, axis=-1, keepdims=True)
                acc = alpha * acc + _dot(p.astype(BF16), v)
                new.append((m_new, l, acc))
            return tuple(new)

        def init(j):
            if use_sink:
                m, l = jnp.full((tq, 1), sink_ref[j] * LOG2E, F32), jnp.ones((tq, 1), F32)
            else:
                m, l = jnp.full((tq, 1), NEG_BIG, F32), jnp.zeros((tq, 1), F32)
            return m, l, jnp.zeros((tq, LANES), F32)

        carry = tuple(init(j) for j in heads)
        start = 0
        for size in chunks:
            carry = step(start, size, carry)
            start += size
        even, odd = (c[2] / c[1] for c in carry)
        o_ref[0, pair] = jnp.where(left, even, odd).astype(BF16)


def _flash(q, k, v, sinks, kidx, vidx, use_sink, tq):
    b, nh, t, _ = q.shape
    nk, tkeys = k.shape[1], k.shape[2]
    nv = v.shape[1]
    return pl.pallas_call(
        functools.partial(_flash_kernel, kidx=kidx, vidx=vidx, tk=FLASH_TK, use_sink=use_sink),
        grid=(b, t // tq),
        in_specs=[pl.BlockSpec(memory_space=pltpu.SMEM),
                  pl.BlockSpec((1, nh, tq, LANES), lambda i, j: (i, 0, j, 0)),
                  pl.BlockSpec((1, nk, tkeys, LANES), lambda i, j: (i, 0, 0, 0)),
                  pl.BlockSpec((1, nv, tkeys, LANES), lambda i, j: (i, 0, 0, 0))],
        out_specs=pl.BlockSpec((1, nh // 2, tq, LANES), lambda i, j: (i, 0, j, 0)),
        out_shape=jax.ShapeDtypeStruct((b, nh // 2, t, LANES), BF16),
        compiler_params=_cparams(("parallel", "parallel")),
        name="flash_attention",
    )(sinks, q, k, v)


def _window_block(sink, q, ks, vs, has_prev, has_next, w):
    k = jnp.concatenate(ks, axis=0)
    v = jnp.concatenate(vs, axis=0)
    s = _dot_nt(q, k)
    qq = lax.broadcasted_iota(jnp.int32, s.shape, 0) & (w - 1)
    col = lax.broadcasted_iota(jnp.int32, s.shape, 1)
    never = 4 * w
    off_prev = jnp.where(has_prev, 0, never)
    off_next = jnp.where(has_next, 0, never)
    valid = ((col < w) & (col >= qq + off_prev)) | ((col >= w) & (col < 2 * w)) \
        | ((col >= 2 * w) & (col < 3 * w) & (col - 2 * w + off_next <= qq)) | (col >= 3 * w)
    s = jnp.where(valid, s, NEG_BIG)
    m = jnp.maximum(sink, jnp.max(s, axis=-1, keepdims=True))
    e = jnp.exp2(s - m)
    denom = jnp.exp2(sink - m) + jnp.sum(e, axis=-1, keepdims=True)
    return _dot(e.astype(BF16), v) / denom


def _window_kernel(sink_ref, q_ref, kp_ref, kc_ref, kn_ref, kx_ref, vp_ref, vc_ref, vn_ref, vx_ref,
                   o_ref):
    nh, w = q_ref.shape[1], kp_ref.shape[2]
    n_win = q_ref.shape[2] // w
    i, steps = pl.program_id(1), pl.num_programs(1)
    sink = jnp.concatenate([jnp.full((w, 1), sink_ref[j] * LOG2E, F32) for j in range(nh)], axis=0)
    ks = [kp_ref[0, 0]] + [kc_ref[0, 0, c * w:(c + 1) * w] for c in range(n_win)] + [kn_ref[0, 0]]
    vs = [vp_ref[0, 0]] + [vc_ref[0, 0, c * w:(c + 1) * w] for c in range(n_win)] + [vn_ref[0, 0]]
    kx, vx = kx_ref[0, 0], vx_ref[0, 0]
    left = _lane((w, LANES)) < HEAD_DIM
    for c in range(n_win):
        rows = slice(c * w, (c + 1) * w)
        has_prev = (i >= 1) if c == 0 else True
        has_next = (i <= steps - 2) if c == n_win - 1 else True
        o = _window_block(sink, q_ref[0, :, rows].reshape(nh * w, LANES), ks[c:c + 3] + [kx],
                          vs[c:c + 3] + [vx], has_prev, has_next, w)
        for p in range(nh // 2):
            o_ref[0, p, rows] = jnp.where(left, o[2 * p * w:(2 * p + 1) * w],
                                          o[(2 * p + 1) * w:(2 * p + 2) * w]).astype(BF16)


def _window_attention(q, k_all, v_all, sinks, n_lat):
    b, nh, _, _ = q.shape
    w = C_WINDOW
    nb = n_lat // w
    n_ctx = k_all.shape[2] - n_lat
    g = WINDOWS_PER_STEP
    assert n_lat % n_ctx == 0 and w & (w - 1) == 0 and nb % g == 0
    prev = pl.BlockSpec((1, 1, w, LANES), lambda i, j: (i, 0, jnp.maximum(g * j - 1, 0), 0))
    cur = pl.BlockSpec((1, 1, g * w, LANES), lambda i, j: (i, 0, j, 0))
    nxt = pl.BlockSpec((1, 1, w, LANES), lambda i, j: (i, 0, jnp.minimum(g * j + g, nb - 1), 0))
    ctx = pl.BlockSpec((1, 1, n_ctx, LANES), lambda i, j: (i, 0, n_lat // n_ctx, 0))
    return pl.pallas_call(
        _window_kernel,
        grid=(b, nb // g),
        in_specs=[pl.BlockSpec(memory_space=pltpu.SMEM),
                  pl.BlockSpec((1, nh, g * w, LANES), lambda i, j: (i, 0, j, 0)),
                  prev, cur, nxt, ctx, prev, cur, nxt, ctx],
        out_specs=pl.BlockSpec((1, nh // 2, g * w, LANES), lambda i, j: (i, 0, j, 0)),
        out_shape=jax.ShapeDtypeStruct((b, nh // 2, n_lat, LANES), BF16),
        compiler_params=_cparams(("parallel", "parallel")),
        name="window_attention",
    )(sinks, q, k_all, k_all, k_all, k_all, v_all, v_all, v_all, v_all)


def _mixed_residual(x_ref, oa_ref, ob_ref, oc_ref, mod, wout_ref):
    att = jnp.concatenate([oa_ref[0, 0], oa_ref[0, 1], oa_ref[0, 2], ob_ref[0, 0], ob_ref[0, 1],
                           oc_ref[0, 0], oc_ref[0, 1], oc_ref[0, 2]], axis=-1)
    return x_ref[0] + mod[2:3] * _dot(att, wout_ref[...])


def _ffn_dense_kernel(x_ref, oa_ref, ob_ref, oc_ref, mod_ref, wout_ref, g_ref, w1_ref, w3_ref, w2_ref,
                      o_ref, *, n_chunks):
    mod = mod_ref[0]
    x1 = _mixed_residual(x_ref, oa_ref, ob_ref, oc_ref, mod, wout_ref)
    hf = ((_rms(x1) * g_ref[...]) * (1.0 + mod[4:5]) + mod[3:4]).astype(BF16)
    fc = w1_ref.shape[1] // n_chunks
    y = jnp.zeros(x1.shape, F32)
    for c in range(n_chunks):
        a = _dot(hf, w1_ref[:, c * fc:(c + 1) * fc])
        gate = _dot(hf, w3_ref[:, c * fc:(c + 1) * fc])
        y = y + _dot((_silu(a) * gate).astype(BF16), w2_ref[c * fc:(c + 1) * fc, :])
    o_ref[0] = x1 + mod[5:6] * y


def _resident(shape):
    return pl.BlockSpec(shape, lambda i, j: (0,) * len(shape), pipeline_mode=pl.Buffered(1))


def _ffn_dense(x, oa, ob, oc, mod, per_batch_mod, lw, tm):
    b, t, d = x.shape
    f = lw["w1"].shape[1]
    mod_map = (lambda i, j: (i, 0, 0)) if per_batch_mod else (lambda i, j: (0, 0, 0))

    def att(n):
        return pl.BlockSpec((1, n, tm, LANES), lambda i, j: (i, 0, j, 0))

    return pl.pallas_call(
        functools.partial(_ffn_dense_kernel, n_chunks=2),
        grid=(b, t // tm),
        in_specs=[pl.BlockSpec((1, tm, d), lambda i, j: (i, j, 0)), att(3), att(2), att(3),
                  pl.BlockSpec((1, 6, d), mod_map),
                  _resident((d, d)), _resident((1, d)),
                  _resident((d, f)), _resident((d, f)), _resident((f, d))],
        out_specs=pl.BlockSpec((1, tm, d), lambda i, j: (i, j, 0)),
        out_shape=jax.ShapeDtypeStruct((b, t, d), F32),
        compiler_params=_cparams(("parallel", "parallel")),
        name="ffn_dense",
    )(x, oa, ob, oc, mod, lw["w_out"], lw["g_ffn"], lw["w1"], lw["w3"], lw["w2"])


def _route_kernel(x_ref, oa_ref, ob_ref, oc_ref, mod_ref, wout_ref, g_ref, wr_ref,
                  x1_ref, hf_ref, ids_ref, gates_ref):
    mod = mod_ref[0]
    x1 = _mixed_residual(x_ref, oa_ref, ob_ref, oc_ref, mod, wout_ref)
    hf = (_rms(x1) * g_ref[...]) * (1.0 + mod[4:5]) + mod[3:4]
    x1_ref[0] = x1
    hf_ref[0] = hf
    logits = _dot3(hf, wr_ref[...])
    lane = _lane(logits.shape)
    logits = jnp.where(lane < N_EXPERTS, logits, NEG_BIG)
    m1 = jnp.max(logits, axis=-1, keepdims=True)
    i1 = jnp.min(jnp.where(logits == m1, lane, LANES), axis=-1, keepdims=True)
    rest = jnp.where(lane == i1, NEG_BIG, logits)
    m2 = jnp.max(rest, axis=-1, keepdims=True)
    i2 = jnp.min(jnp.where(rest == m2, lane, LANES), axis=-1, keepdims=True)
    e2 = jnp.exp(m2 - m1)
    g1 = 1.0 / (1.0 + e2)
    ids_ref[0] = jnp.where(lane == 0, i1, jnp.where(lane == 1, i2, 0))
    gates_ref[0] = jnp.where(lane == 0, g1, jnp.where(lane == 1, e2 * g1, 0.0))


def _route(x, oa, ob, oc, mod, lw, tm):
    b, t, d = x.shape

    def att(n):
        return pl.BlockSpec((1, n, tm, LANES), lambda i, j: (i, 0, j, 0))

    def const(shape):
        return pl.BlockSpec(shape, lambda i, j: (0,) * len(shape))

    tok = pl.BlockSpec((1, tm, d), lambda i, j: (i, j, 0))
    meta = pl.BlockSpec((1, tm, LANES), lambda i, j: (i, j, 0))
    return pl.pallas_call(
        _route_kernel,
        grid=(b, t // tm),
        in_specs=[tok, att(3), att(2), att(3), pl.BlockSpec((1, 6, d), lambda i, j: (i, 0, 0)),
                  const((d, d)), const((1, d)), const((d, LANES))],
        out_specs=[tok, tok, meta, meta],
        out_shape=[jax.ShapeDtypeStruct((b, t, d), F32), jax.ShapeDtypeStruct((b, t, d), F32),
                   jax.ShapeDtypeStruct((b, t, LANES), jnp.int32),
                   jax.ShapeDtypeStruct((b, t, LANES), F32)],
        compiler_params=_cparams(("parallel", "parallel")),
        name="moe_route",
    )(x, oa, ob, oc, mod, lw["w_out"], lw["g_ffn"], lw["w_router"])


def _row_copy(src_ref, src_row, dst_ref, dst_row, sem):
    return pltpu.make_async_copy(src_ref.at[pl.ds(src_row, 1)], dst_ref.at[pl.ds(dst_row, 1)], sem)


def _load_indices(idx_ref, tile, idx_smem, sem):
    cp = pltpu.make_async_copy(idx_ref.at[tile], idx_smem, sem)
    cp.start()
    cp.wait()


def _wait_rows(src_ref, dst_ref, sem):
    pltpu.make_async_copy(src_ref.at[pl.ds(0, dst_ref.shape[0])], dst_ref, sem).wait()


def _experts_kernel(te_ref, tv_ref, src_ref, hf_ref, w1_ref, w3_ref, w2_ref, y_ref,
                    src_smem, xg_scr, xb_scr, acc_scr, src_sem, sems):
    i, f = pl.program_id(0), pl.program_id(1)
    tm = xb_scr.shape[0]

    def gather(tile):
        slot = tile % 2
        _load_indices(src_ref, tile, src_smem, src_sem)

        def issue(r, carry):
            _row_copy(hf_ref, src_smem[r], xg_scr.at[slot], r, sems.at[slot]).start()
            return carry

        lax.fori_loop(0, tm, issue, 0, unroll=ISSUE_UNROLL)

    @pl.when((f == 0) & (i == 0))
    def _():
        gather(0)

    @pl.when((f == 0) & (tv_ref[i] > 0))
    def _():
        slot = i % 2
        _wait_rows(hf_ref, xg_scr.at[slot], sems.at[slot])
        xb_scr[...] = xg_scr[slot].astype(BF16)

    @pl.when(f == 0)
    def _():
        acc_scr[...] = jnp.zeros(acc_scr.shape, F32)
        nxt = jnp.minimum(i + 1, pl.num_programs(0) - 1)

        @pl.when((i + 1 < pl.num_programs(0)) & (tv_ref[nxt] > 0))
        def _():
            gather(i + 1)

    @pl.when(tv_ref[i] > 0)
    def _():
        xb = xb_scr[...]
        a = _dot(xb, w1_ref[...])
        gate = _dot(xb, w3_ref[...])
        acc_scr[...] += _dot((_silu(a) * gate).astype(BF16), w2_ref[...])

    @pl.when(f == pl.num_programs(1) - 1)
    def _():
        y_ref[...] = acc_scr[...]


def _experts(tile_expert, tile_valid, src, hf, lw, tf):
    n_tiles, tm = src.shape
    d = hf.shape[1]
    f = lw["w1"].shape[2]
    grid_spec = pltpu.PrefetchScalarGridSpec(
        num_scalar_prefetch=2,
        grid=(n_tiles, f // tf),
        in_specs=[pl.BlockSpec(memory_space=pltpu.VMEM), pl.BlockSpec(memory_space=pl.ANY),
                  pl.BlockSpec((None, d, tf), lambda i, k, te, tv: (te[i], 0, k)),
                  pl.BlockSpec((None, d, tf), lambda i, k, te, tv: (te[i], 0, k)),
                  pl.BlockSpec((None, tf, d), lambda i, k, te, tv: (te[i], k, 0))],
        out_specs=pl.BlockSpec((tm, d), lambda i, k, te, tv: (i, 0)),
        scratch_shapes=[pltpu.SMEM((tm,), jnp.int32), pltpu.VMEM((2, tm, d), F32),
                        pltpu.VMEM((tm, d), BF16), pltpu.VMEM((tm, d), F32),
                        pltpu.SemaphoreType.DMA(()), pltpu.SemaphoreType.DMA((2,))])
    return pl.pallas_call(
        _experts_kernel,
        grid_spec=grid_spec,
        out_shape=jax.ShapeDtypeStruct((n_tiles * tm, d), F32),
        compiler_params=_cparams(("arbitrary", "arbitrary")),
        name="moe_experts",
    )(tile_expert, tile_valid, src, hf, lw["w1"], lw["w3"], lw["w2"])


def _combine_kernel(pos_ref, y_ref, x1_ref, gates_ref, mod_ref, gfin_ref, o_ref,
                    pos_smem, y0_scr, y1_scr, pos_sem, sem):
    rows = y0_scr.shape[0]
    _load_indices(pos_ref, pl.program_id(0), pos_smem, pos_sem)

    def issue(r, carry):
        _row_copy(y_ref, pos_smem[r], y0_scr, r, sem).start()
        _row_copy(y_ref, pos_smem[rows + r], y1_scr, r, sem).start()
        return carry

    lax.fori_loop(0, rows, issue, 0, unroll=ISSUE_UNROLL)
    _wait_rows(y_ref, y0_scr, sem)
    _wait_rows(y_ref, y1_scr, sem)
    gates = gates_ref[...]
    y = gates[:, 0:1] * y0_scr[...] + gates[:, 1:2] * y1_scr[...]
    o_ref[...] = _rms(x1_ref[...] + mod_ref[0, 5:6] * y) * gfin_ref[...]


def _combine(pos, y, x1, gates, mod, gfin, tokens_per_batch):
    n, d = x1.shape
    n_tiles, two_rows = pos.shape
    rows = two_rows // 2
    tok = pl.BlockSpec((rows, d), lambda i: (i, 0))
    return pl.pallas_call(
        _combine_kernel,
        grid=(n_tiles,),
        in_specs=[pl.BlockSpec(memory_space=pltpu.VMEM), pl.BlockSpec(memory_space=pl.ANY), tok,
                  pl.BlockSpec((rows, LANES), lambda i: (i, 0)),
                  pl.BlockSpec((1, 6, d), lambda i: (i * rows // tokens_per_batch, 0, 0)),
                  pl.BlockSpec((1, d), lambda i: (0, 0))],
        out_specs=tok,
        out_shape=jax.ShapeDtypeStruct((n, d), F32),
        scratch_shapes=[pltpu.SMEM((two_rows,), jnp.int32), pltpu.VMEM((rows, d), F32),
                        pltpu.VMEM((rows, d), F32), pltpu.SemaphoreType.DMA(()),
                        pltpu.SemaphoreType.DMA(())],
        compiler_params=_cparams(("arbitrary",)),
        name="moe_combine",
    )(pos, y, x1, gates, mod, gfin)


def _dispatch_plan(ids, tm, rows):
    n = ids.shape[0]
    flat = ids.reshape(-1)
    onehot = (flat[None, :] == jnp.arange(N_EXPERTS)[:, None]).astype(jnp.int32)
    csum = jnp.cumsum(onehot, axis=1)
    rank = jnp.sum(csum * onehot, axis=0) - 1
    counts = csum[:, -1]
    padded = (counts + tm - 1) // tm * tm
    ends = jnp.cumsum(padded)
    pos = (ends - padded)[flat] + rank
    n_sorted = 2 * n + N_EXPERTS * tm
    starts = jnp.arange(n_sorted // tm, dtype=jnp.int32) * tm
    tile_expert = jnp.sum((starts[:, None] >= ends[None, :]).astype(jnp.int32), axis=1)
    tile_valid = (starts < ends[-1]).astype(jnp.int32)
    last_expert = tile_expert[jnp.maximum(ends[-1] // tm - 1, 0)]
    tile_expert = jnp.where(tile_valid > 0, tile_expert, last_expert)
    order = jnp.argsort(flat, stable=True).astype(jnp.int32)
    first = jnp.cumsum(counts) - counts
    row = jnp.arange(n_sorted, dtype=jnp.int32)
    row_expert = jnp.repeat(tile_expert, tm)
    r = row - (ends - padded)[row_expert]
    live = (r < counts[row_expert]) & (jnp.repeat(tile_valid, tm) > 0)
    src = jnp.where(live, order[jnp.clip(first[row_expert] + r, 0, 2 * n - 1)] // 2, 0)
    pos = pos.reshape(n // rows, rows, 2).transpose(0, 2, 1).reshape(n // rows, 2 * rows)
    return pos.astype(jnp.int32), src.reshape(n_sorted // tm, tm), tile_expert, tile_valid


def _ffn_moe(x, oa, ob, oc, mod, lw, gfin):
    b, t, d = x.shape
    n = b * t
    assert t % ROW_TILE == 0
    x1, hf, ids, gates = _route(x, oa, ob, oc, mod, lw, 512)
    pos, src, tile_expert, tile_valid = _dispatch_plan(ids.reshape(n, LANES)[:, :2], MOE_TM, ROW_TILE)
    y = _experts(tile_expert, tile_valid, src, hf.reshape(n, d), lw, MOE_TF)
    out = _combine(pos, y, x1.reshape(n, d), gates.reshape(n, LANES), mod, gfin, t)
    return out.reshape(b, t, d)


def _rope_tables(n_lat):
    rows = n_lat // GRID_W
    row = jnp.repeat(jnp.arange(rows, dtype=F32), GRID_W)
    col = jnp.tile(jnp.arange(GRID_W, dtype=F32), rows)

    def table(rot_dim):
        nf = rot_dim // 4
        inv = ROPE_THETA ** (-jnp.arange(nf, dtype=F32) / nf)
        ang_r, ang_c = row[:, None] * inv, col[:, None] * inv
        ang = jnp.concatenate([ang_r, ang_r, ang_c, ang_c], axis=-1)
        sign = jnp.tile(jnp.concatenate([-jnp.ones(nf, F32), jnp.ones(nf, F32)]), 2)
        return jnp.cos(ang), jnp.sin(ang) * sign

    cos_h, sin_h = table(HEAD_DIM)
    cos_r, sin_r = table(B_ROPE)
    one = jnp.ones((n_lat, B_NOPE), F32)
    pad = LANES - B_NOPE - B_ROPE
    cos_b = jnp.concatenate([one, cos_r, jnp.ones((n_lat, pad), F32)], axis=-1)
    sin_b = jnp.concatenate([0 * one, sin_r, jnp.zeros((n_lat, pad), F32)], axis=-1)
    return (jnp.tile(cos_h, (1, 2)), jnp.tile(sin_h, (1, 2)), cos_b, sin_b)


def _permute_heads(w, axis):
    shape = w.shape
    w = w.reshape(shape[:axis] + (6, HEAD_DIM) + shape[axis + 1:])
    w = jnp.take(w, jnp.array(HEAD_PERM), axis=axis)
    return w.reshape(shape)


def _layer_weights(i, w_in, norm_ffn, a_q_norm, a_k_norm, b_q_norm, b_kv_norm, w_uq, w_ukv, w_out):
    d = w_in.shape[1]
    wi = w_in[i]
    qk = A_HEADS * HEAD_DIM
    kv = A_KV * HEAD_DIM
    b0 = A_COLS
    c0 = A_COLS + B_COLS
    k_rope = jnp.pad(wi[:, b0 + B_Q_RANK + B_KV_RANK:c0], ((0, 0), (B_NOPE, LANES - B_NOPE - B_ROPE)))
    w_in_p = jnp.concatenate([
        _permute_heads(wi[:, :qk], 1), wi[:, qk:A_COLS],
        wi[:, b0:b0 + B_Q_RANK + B_KV_RANK], k_rope,
        _permute_heads(wi[:, c0:c0 + qk], 1), wi[:, c0 + qk:]], axis=1).astype(BF16)
    w_uq_p = jnp.pad(w_uq[i].reshape(B_Q_RANK, B_HEADS, B_NOPE + B_ROPE),
                     ((0, 0), (0, 0), (0, LANES - B_NOPE - B_ROPE))).reshape(B_Q_RANK, B_HEADS * LANES)
    ukv = w_ukv[i].reshape(B_KV_RANK, B_HEADS, B_NOPE + B_V)
    k_up = jnp.pad(ukv[:, :, :B_NOPE], ((0, 0), (0, 0), (0, LANES - B_NOPE))).reshape(B_KV_RANK, -1)
    v_up = ukv[:, :, B_NOPE:].reshape(B_KV_RANK, -1)
    wo = w_out[i]
    w_out_p = jnp.concatenate([_permute_heads(wo[:qk], 0), wo[qk:qk + B_HEADS * B_V],
                               _permute_heads(wo[qk + B_HEADS * B_V:], 0)], axis=0).astype(BF16)
    blk = (jnp.arange(LANES)[:, None] // HEAD_DIM) == (jnp.arange(LANES)[None, :] // HEAD_DIM)
    del kv, d
    return {
        "w_in": w_in_p,
        "m2": (blk.astype(F32) / HEAD_DIM).astype(BF16),
        "a_q_gain": jnp.tile(a_q_norm[i], 2)[None], "a_k_gain": jnp.tile(a_k_norm[i], 2)[None],
        "b_q_gain": b_q_norm[i][None], "b_kv_gain": b_kv_norm[i][None],
        "w_uq": w_uq_p.astype(BF16),
        "w_ukv": jnp.concatenate([k_up, v_up], axis=1).astype(BF16),
        "w_out": w_out_p,
        "g_ffn": norm_ffn[i][None],
    }


def kernel(x, c, ctx, c_ctx, w_mod, b_mod, norm_mix, norm_ffn, w_in, a_q_norm, a_k_norm, b_q_norm,
           b_kv_norm, w_uq, w_ukv, c_sinks, w_out, w1_dense, w3_dense, w2_dense, w_router, w1_moe,
           w3_moe, w2_moe, final_norm):
    bsz, n_lat, d = x.shape
    n_ctx = ctx.shape[1]
    depth = w_mod.shape[0]
    mod_rows = 16
    cc = jnp.concatenate([c, c_ctx[None], jnp.zeros((mod_rows - bsz - 1, d), F32)], axis=0)
    mods = _modulation(cc, w_mod, b_mod)
    tables = _rope_tables(n_lat)
    no_sink = jnp.zeros((A_HEADS,), F32)
    xc = ctx
    for i in range(depth):
        last = i == depth - 1
        mod_lat = mods[i, :bsz].reshape(bsz, 6, d)
        mod_ctx = mods[i, bsz:bsz + 1].reshape(1, 6, d)
        lw = _layer_weights(i, w_in, norm_ffn, a_q_norm, a_k_norm, b_q_norm, b_kv_norm, w_uq, w_ukv,
                            w_out)
        g_mix = norm_mix[i][None]
        lat = _in_projection(x, mod_lat, True, g_mix, lw, tables, True, 512)
        cx = _in_projection(xc, mod_ctx, False, g_mix, lw, tables, False, n_ctx)
        qa, ka, va, qb, kb, vb, qc, kc, vc = lat
        cqa, cka, cva, cqb, ckb, cvb, cqc, ckc, cvc = cx

        def both(u, w):
            return jnp.concatenate([u, w], axis=2)

        sinks = jnp.take(c_sinks[i], jnp.array(HEAD_PERM))
        oa = _flash(qa, both(ka, cka), both(va, cva), no_sink, (0,) * 6, (0,) * 6, False, 512)
        ob = _flash(qb, both(kb, ckb), both(vb, cvb), no_sink, (0, 1, 2, 3), (0, 0, 1, 1), False, 512)
        oc = _window_attention(qc, both(kc, ckc), both(vc, cvc), sinks, n_lat)
        if not last:
            coa = _flash(cqa, cka, cva, no_sink, (0,) * 6, (0,) * 6, False, n_ctx)
            cob = _flash(cqb, ckb, cvb, no_sink, (0, 1, 2, 3), (0, 0, 1, 1), False, n_ctx)
            coc = _flash(cqc, ckc, cvc, sinks, (0,) * 6, (0,) * 6, True, n_ctx)
        j = i // 2
        if i % 2 == 0:
            lw.update(w1=w1_dense[j].astype(BF16), w3=w3_dense[j].astype(BF16),
                      w2=w2_dense[j].astype(BF16))
            x_new = _ffn_dense(x, oa, ob, oc, mod_lat, True, lw, 512)
            if not last:
                xc = _ffn_dense(xc, coa, cob, coc, mod_ctx, False, lw, n_ctx)
            x = x_new
        else:
            assert last, "the expert channel mixer is fused with the final norm"
            lw.update(w1=w1_moe[j].astype(BF16), w3=w3_moe[j].astype(BF16), w2=w2_moe[j].astype(BF16),
                      w_router=jnp.pad(w_router[j], ((0, 0), (0, LANES - N_EXPERTS))))
            x = _ffn_moe(x, oa, ob, oc, mod_lat, lw, final_norm[None])
    return x
```

```python
import functools

import jax
import jax.numpy as jnp
from jax import lax
from jax.experimental import pallas as pl
from jax.experimental.pallas import tpu as pltpu

F32 = jnp.float32
BF16 = jnp.bfloat16

LANES = 128
HEAD_DIM = 64
GRID_W = 64
ROPE_THETA = 10000.0
EPS = 1e-6
NEG_BIG = -1e30
A_HEADS, A_KV = 6, 2
B_HEADS, B_Q_RANK, B_KV_RANK, B_NOPE, B_ROPE, B_V = 4, 256, 128, 64, 32, 64
C_HEADS, C_KV, C_WINDOW = 6, 2, 128
N_EXPERTS = 8
A_COLS = (A_HEADS + 2 * A_KV) * HEAD_DIM
B_COLS = B_Q_RANK + B_KV_RANK + B_ROPE
VMEM_LIMIT = 56 * 1024 * 1024
LOG2E = 1.4426950408889634
FLASH_TK = 2304
MXU_COLS = 256
WINDOWS_PER_STEP = 8
ISSUE_UNROLL = 8
MOE_TM = 1024
MOE_TF = 896
ROW_TILE = 1024

HEAD_PERM = tuple((j % 2) * 3 + j // 2 for j in range(6))


def _cparams(sem):
    return pltpu.CompilerParams(dimension_semantics=sem, vmem_limit_bytes=VMEM_LIMIT)


def _rms(x):
    return x * lax.rsqrt(jnp.mean(x * x, axis=-1, keepdims=True) + EPS)


def _dot(a, b):
    return jnp.dot(a, b, preferred_element_type=F32)


def _dot_nt(a, b):
    return lax.dot_general(a, b, (((1,), (1,)), ((), ())), preferred_element_type=F32)


def _split(a):
    hi = a.astype(BF16)
    return hi, (a - hi.astype(F32)).astype(BF16)


def _dot3(a, b):
    ah, al = _split(a)
    bh, bl = _split(b)
    return _dot(ah, bh) + (_dot(ah, bl) + _dot(al, bh))


def _silu(a):
    return a * jax.nn.sigmoid(a)


def _lane(shape):
    return lax.broadcasted_iota(jnp.int32, shape, len(shape) - 1)


def _mod_kernel(c_ref, w_ref, b_ref, o_ref):
    o_ref[...] = _dot3(_silu(c_ref[...]), w_ref[...]) + b_ref[...]


def _modulation(cc, w_mod, b_mod):
    depth, d, n = w_mod.shape
    rows = cc.shape[0]
    tn = 1536
    return pl.pallas_call(
        _mod_kernel,
        grid=(depth, n // tn),
        in_specs=[pl.BlockSpec((rows, d), lambda i, j: (0, 0)),
                  pl.BlockSpec((None, d, tn), lambda i, j: (i, 0, j)),
                  pl.BlockSpec((None, 1, tn), lambda i, j: (i, 0, j))],
        out_specs=pl.BlockSpec((None, rows, tn), lambda i, j: (i, 0, j)),
        out_shape=jax.ShapeDtypeStruct((depth, rows, n), F32),
        compiler_params=_cparams(("arbitrary", "arbitrary")),
        name="modulation",
    )(cc, w_mod, b_mod.reshape(depth, 1, n))


def _rope(t, cos, sin_signed, half):
    first = (_lane(t.shape) & (2 * half - 1)) < half
    rot = jnp.where(first, pltpu.roll(t, LANES - half, 1), pltpu.roll(t, half, 1))
    return t * cos + rot * sin_signed


def _store_pair(ref, p, rows, t):
    left = _lane(t.shape) < HEAD_DIM
    ref[0, 2 * p, rows] = jnp.where(left, t, 0.0).astype(BF16)
    ref[0, 2 * p + 1, rows] = jnp.where(left, 0.0, t).astype(BF16)


def _proj_kernel(*refs, use_rope, n_parts):
    part_rows = refs[0].shape[1] // n_parts
    for part in range(n_parts):
        _proj_rows(slice(part * part_rows, (part + 1) * part_rows), use_rope, *refs)


def _proj_rows(rows, use_rope, x_ref, mod_ref, g_ref, win_ref, m2_ref, gaq_ref, gak_ref, gbq_ref,
               gbkv_ref, wuq_ref, wukv_ref, cosa_ref, sina_ref, cosb_ref, sinb_ref,
               qa_ref, ka_ref, va_ref, qb_ref, kb_ref, vb_ref, qc_ref, kc_ref, vc_ref):
    mod = mod_ref[0]
    h = _rms(x_ref[0, rows]) * g_ref[...]
    h = h * (1.0 + mod[1:2]) + mod[0:1]
    proj = _dot(h.astype(BF16), win_ref[...])

    def chunk(i):
        return proj[:, i * LANES:(i + 1) * LANES]

    def head_norm(t, gain):
        ms = _dot((t * t).astype(BF16), m2_ref[...])
        return t * lax.rsqrt(ms + EPS) * gain

    def rope_a(t):
        return _rope(t, cosa_ref[rows], sina_ref[rows], HEAD_DIM // 4) if use_rope else t

    def rope_b(t):
        return _rope(t, cosb_ref[rows], sinb_ref[rows], B_ROPE // 4) if use_rope else t

    scale = HEAD_DIM ** -0.5 * LOG2E
    for p in range(3):
        _store_pair(qa_ref, p, rows, rope_a(head_norm(chunk(p), gaq_ref[...])) * scale)
    ka_ref[0, 0, rows] = rope_a(head_norm(chunk(3), gak_ref[...])).astype(BF16)
    va_ref[0, 0, rows] = chunk(4).astype(BF16)
    cq = _rms(proj[:, 5 * LANES:7 * LANES]) * gbq_ref[...]
    qb = _dot(cq.astype(BF16), wuq_ref[...])
    ckv = _rms(chunk(7)) * gbkv_ref[...]
    kvb = _dot(ckv.astype(BF16), wukv_ref[...])
    k_rope = rope_b(chunk(8))
    scale_b = (B_NOPE + B_ROPE) ** -0.5 * LOG2E
    for hd in range(B_HEADS):
        qb_ref[0, hd, rows] = (rope_b(qb[:, hd * LANES:(hd + 1) * LANES]) * scale_b).astype(BF16)
        kb_ref[0, hd, rows] = (kvb[:, hd * LANES:(hd + 1) * LANES] + k_rope).astype(BF16)
    for p in range(2):
        vb_ref[0, p, rows] = kvb[:, (B_HEADS + p) * LANES:(B_HEADS + p + 1) * LANES].astype(BF16)
    for p in range(3):
        _store_pair(qc_ref, p, rows, rope_a(chunk(9 + p)) * scale)
    kc_ref[0, 0, rows] = rope_a(chunk(12)).astype(BF16)
    vc_ref[0, 0, rows] = chunk(13).astype(BF16)


def _in_projection(x, mod, per_batch_mod, g, lw, tables, use_rope, tm):
    b, t, d = x.shape
    ncol = lw["w_in"].shape[1]
    heads = (6, 1, 1, 4, 4, 2, 6, 1, 1)

    def const(shape):
        return pl.BlockSpec(shape, lambda i, j: (0,) * len(shape))

    mod_map = (lambda i, j: (i, 0, 0)) if per_batch_mod else (lambda i, j: (0, 0, 0))
    tab = pl.BlockSpec((tm, LANES), lambda i, j: (j, 0))
    return pl.pallas_call(
        functools.partial(_proj_kernel, use_rope=use_rope, n_parts=2),
        grid=(b, t // tm),
        in_specs=[pl.BlockSpec((1, tm, d), lambda i, j: (i, j, 0)),
                  pl.BlockSpec((1, 6, d), mod_map),
                  const((1, d)), const((d, ncol)), const((LANES, LANES)),
                  const((1, LANES)), const((1, LANES)), const((1, B_Q_RANK)), const((1, B_KV_RANK)),
                  const((B_Q_RANK, 4 * LANES)), const((B_KV_RANK, 6 * LANES)),
                  tab, tab, tab, tab],
        out_specs=[pl.BlockSpec((1, n, tm, LANES), lambda i, j: (i, 0, j, 0)) for n in heads],
        out_shape=[jax.ShapeDtypeStruct((b, n, t, LANES), BF16) for n in heads],
        compiler_params=_cparams(("parallel", "parallel")),
        name="in_projection",
    )(x, mod, g, lw["w_in"], lw["m2"], lw["a_q_gain"], lw["a_k_gain"], lw["b_q_gain"],
      lw["b_kv_gain"], lw["w_uq"], lw["w_ukv"], *tables)


def _key_chunks(t_keys, max_chunk):
    tiles = -(-t_keys // MXU_COLS)
    n = -(-tiles * MXU_COLS // max_chunk)
    sizes = [(tiles // n + (c < tiles % n)) * MXU_COLS for c in range(n)]
    sizes[-1] -= tiles * MXU_COLS - t_keys
    return tuple(sizes)


def _flash_kernel(sink_ref, q_ref, k_ref, v_ref, o_ref, *, kidx, vidx, tk, use_sink):
    nh, tq = q_ref.shape[1], q_ref.shape[2]
    t_keys = k_ref.shape[2]
    chunks = _key_chunks(t_keys, tk)
    left = _lane((tq, LANES)) < HEAD_DIM
    for pair in range(nh // 2):
        heads = (2 * pair, 2 * pair + 1)

        def step(start, size, carry, heads=heads):
            new = []
            for j, (m, l, acc) in zip(heads, carry):
                k = k_ref[0, kidx[j], pl.ds(start, size), :]
                v = v_ref[0, vidx[j], pl.ds(start, size), :]
                s = _dot_nt(q_ref[0, j], k)
                m_new = jnp.maximum(m, jnp.max(s, axis=-1, keepdims=True))
                alpha = jnp.exp2(m - m_new)
                p = jnp.exp2(s - m_new)
                l = alpha * l + jnp.sum(p, axis=-1, keepdims=True)
                acc = alpha * acc + _dot(p.astype(BF16), v)
                new.append((m_new, l, acc))
            return tuple(new)

        def init(j):
            if use_sink:
                m, l = jnp.full((tq, 1), sink_ref[j] * LOG2E, F32), jnp.ones((tq, 1), F32)
            else:
                m, l = jnp.full((tq, 1), NEG_BIG, F32), jnp.zeros((tq, 1), F32)
            return m, l, jnp.zeros((tq, LANES), F32)

        carry = tuple(init(j) for j in heads)
        start = 0
        for size in chunks:
            carry = step(start, size, carry)
            start += size
        even, odd = (c[2] / c[1] for c in carry)
        o_ref[0, pair] = jnp.where(left, even, odd).astype(BF16)


def _flash(q, k, v, sinks, kidx, vidx, use_sink, tq):
    b, nh, t, _ = q.shape
    nk, tkeys = k.shape[1], k.shape[2]
    nv = v.shape[1]
    return pl.pallas_call(
        functools.partial(_flash_kernel, kidx=kidx, vidx=vidx, tk=FLASH_TK, use_sink=use_sink),
        grid=(b, t // tq),
        in_specs=[pl.BlockSpec(memory_space=pltpu.SMEM),
                  pl.BlockSpec((1, nh, tq, LANES), lambda i, j: (i, 0, j, 0)),
                  pl.BlockSpec((1, nk, tkeys, LANES), lambda i, j: (i, 0, 0, 0)),
                  pl.BlockSpec((1, nv, tkeys, LANES), lambda i, j: (i, 0, 0, 0))],
        out_specs=pl.BlockSpec((1, nh // 2, tq, LANES), lambda i, j: (i, 0, j, 0)),
        out_shape=jax.ShapeDtypeStruct((b, nh // 2, t, LANES), BF16),
        compiler_params=_cparams(("parallel", "parallel")),
        name="flash_attention",
    )(sinks, q, k, v)


def _window_block(sink, q, ks, vs, has_prev, has_next, w):
    k = jnp.concatenate(ks, axis=0)
    v = jnp.concatenate(vs, axis=0)
    s = _dot_nt(q, k)
    qq = lax.broadcasted_iota(jnp.int32, s.shape, 0) & (w - 1)
    col = lax.broadcasted_iota(jnp.int32, s.shape, 1)
    never = 4 * w
    off_prev = jnp.where(has_prev, 0, never)
    off_next = jnp.where(has_next, 0, never)
    valid = ((col < w) & (col >= qq + off_prev)) | ((col >= w) & (col < 2 * w)) \
        | ((col >= 2 * w) & (col < 3 * w) & (col - 2 * w + off_next <= qq)) | (col >= 3 * w)
    s = jnp.where(valid, s, NEG_BIG)
    m = jnp.maximum(sink, jnp.max(s, axis=-1, keepdims=True))
    e = jnp.exp2(s - m)
    denom = jnp.exp2(sink - m) + jnp.sum(e, axis=-1, keepdims=True)
    return _dot(e.astype(BF16), v) / denom


def _window_kernel(sink_ref, q_ref, kp_ref, kc_ref, kn_ref, kx_ref, vp_ref, vc_ref, vn_ref, vx_ref,
                   o_ref):
    nh, w = q_ref.shape[1], kp_ref.shape[2]
    n_win = q_ref.shape[2] // w
    i, steps = pl.program_id(1), pl.num_programs(1)
    sink = jnp.concatenate([jnp.full((w, 1), sink_ref[j] * LOG2E, F32) for j in range(nh)], axis=0)
    ks = [kp_ref[0, 0]] + [kc_ref[0, 0, c * w:(c + 1) * w] for c in range(n_win)] + [kn_ref[0, 0]]
    vs = [vp_ref[0, 0]] + [vc_ref[0, 0, c * w:(c + 1) * w] for c in range(n_win)] + [vn_ref[0, 0]]
    kx, vx = kx_ref[0, 0], vx_ref[0, 0]
    left = _lane((w, LANES)) < HEAD_DIM
    for c in range(n_win):
        rows = slice(c * w, (c + 1) * w)
        has_prev = (i >= 1) if c == 0 else True
        has_next = (i <= steps - 2) if c == n_win - 1 else True
        o = _window_block(sink, q_ref[0, :, rows].reshape(nh * w, LANES), ks[c:c + 3] + [kx],
                          vs[c:c + 3] + [vx], has_prev, has_next, w)
        for p in range(nh // 2):
            o_ref[0, p, rows] = jnp.where(left, o[2 * p * w:(2 * p + 1) * w],
                                          o[(2 * p + 1) * w:(2 * p + 2) * w]).astype(BF16)


def _window_attention(q, k_all, v_all, sinks, n_lat):
    b, nh, _, _ = q.shape
    w = C_WINDOW
    nb = n_lat // w
    n_ctx = k_all.shape[2] - n_lat
    g = max(c for c in range(1, WINDOWS_PER_STEP + 1) if nb % c == 0)
    assert n_lat % n_ctx == 0 and w & (w - 1) == 0
    prev = pl.BlockSpec((1, 1, w, LANES), lambda i, j: (i, 0, jnp.maximum(g * j - 1, 0), 0))
    cur = pl.BlockSpec((1, 1, g * w, LANES), lambda i, j: (i, 0, j, 0))
    nxt = pl.BlockSpec((1, 1, w, LANES), lambda i, j: (i, 0, jnp.minimum(g * j + g, nb - 1), 0))
    ctx = pl.BlockSpec((1, 1, n_ctx, LANES), lambda i, j: (i, 0, n_lat // n_ctx, 0))
    return pl.pallas_call(
        _window_kernel,
        grid=(b, nb // g),
        in_specs=[pl.BlockSpec(memory_space=pltpu.SMEM),
                  pl.BlockSpec((1, nh, g * w, LANES), lambda i, j: (i, 0, j, 0)),
                  prev, cur, nxt, ctx, prev, cur, nxt, ctx],
        out_specs=pl.BlockSpec((1, nh // 2, g * w, LANES), lambda i, j: (i, 0, j, 0)),
        out_shape=jax.ShapeDtypeStruct((b, nh // 2, n_lat, LANES), BF16),
        compiler_params=_cparams(("parallel", "parallel")),
        name="window_attention",
    )(sinks, q, k_all, k_all, k_all, k_all, v_all, v_all, v_all, v_all)


def _mixed_residual(x_ref, oa_ref, ob_ref, oc_ref, mod, wout_ref, rows=slice(None)):
    att = jnp.concatenate([oa_ref[0, 0, rows], oa_ref[0, 1, rows], oa_ref[0, 2, rows],
                           ob_ref[0, 0, rows], ob_ref[0, 1, rows],
                           oc_ref[0, 0, rows], oc_ref[0, 1, rows], oc_ref[0, 2, rows]], axis=-1)
    return x_ref[0, rows] + mod[2:3] * _dot(att, wout_ref[...])


def _ffn_dense_kernel(x_ref, oa_ref, ob_ref, oc_ref, mod_ref, wout_ref, g_ref, w1_ref, w3_ref, w2_ref,
                      o_ref, *, n_chunks):
    mod = mod_ref[0]
    x1 = _mixed_residual(x_ref, oa_ref, ob_ref, oc_ref, mod, wout_ref)
    hf = ((_rms(x1) * g_ref[...]) * (1.0 + mod[4:5]) + mod[3:4]).astype(BF16)
    fc = w1_ref.shape[1] // n_chunks
    y = jnp.zeros(x1.shape, F32)
    for c in range(n_chunks):
        a = _dot(hf, w1_ref[:, c * fc:(c + 1) * fc])
        gate = _dot(hf, w3_ref[:, c * fc:(c + 1) * fc])
        y = y + _dot((_silu(a) * gate).astype(BF16), w2_ref[c * fc:(c + 1) * fc, :])
    o_ref[0] = x1 + mod[5:6] * y


def _resident(shape):
    return pl.BlockSpec(shape, lambda i, j: (0,) * len(shape), pipeline_mode=pl.Buffered(1))


def _ffn_dense(x, oa, ob, oc, mod, per_batch_mod, lw, tm):
    b, t, d = x.shape
    f = lw["w1"].shape[1]
    mod_map = (lambda i, j: (i, 0, 0)) if per_batch_mod else (lambda i, j: (0, 0, 0))

    def att(n):
        return pl.BlockSpec((1, n, tm, LANES), lambda i, j: (i, 0, j, 0))

    return pl.pallas_call(
        functools.partial(_ffn_dense_kernel, n_chunks=2),
        grid=(b, t // tm),
        in_specs=[pl.BlockSpec((1, tm, d), lambda i, j: (i, j, 0)), att(3), att(2), att(3),
                  pl.BlockSpec((1, 6, d), mod_map),
                  _resident((d, d)), _resident((1, d)),
                  _resident((d, f)), _resident((d, f)), _resident((f, d))],
        out_specs=pl.BlockSpec((1, tm, d), lambda i, j: (i, j, 0)),
        out_shape=jax.ShapeDtypeStruct((b, t, d), F32),
        compiler_params=_cparams(("parallel", "parallel")),
        name="ffn_dense",
    )(x, oa, ob, oc, mod, lw["w_out"], lw["g_ffn"], lw["w1"], lw["w3"], lw["w2"])


def _route_kernel(x_ref, oa_ref, ob_ref, oc_ref, mod_ref, wout_ref, g_ref, wr_ref,
                  x1_ref, hf_ref, ids_ref, gates_ref):
    half = x_ref.shape[1] // 2
    for part in range(2):
        _route_rows(slice(part * half, (part + 1) * half), x_ref, oa_ref, ob_ref, oc_ref, mod_ref,
                    wout_ref, g_ref, wr_ref, x1_ref, hf_ref, ids_ref, gates_ref)


def _route_rows(rows, x_ref, oa_ref, ob_ref, oc_ref, mod_ref, wout_ref, g_ref, wr_ref,
                x1_ref, hf_ref, ids_ref, gates_ref):
    mod = mod_ref[0]
    x1 = _mixed_residual(x_ref, oa_ref, ob_ref, oc_ref, mod, wout_ref, rows)
    hf = (_rms(x1) * g_ref[...]) * (1.0 + mod[4:5]) + mod[3:4]
    x1_ref[0, rows] = x1
    hf_ref[0, rows] = hf
    logits = _dot3(hf, wr_ref[...])
    lane = _lane(logits.shape)
    logits = jnp.where(lane < N_EXPERTS, logits, NEG_BIG)
    m1 = jnp.max(logits, axis=-1, keepdims=True)
    i1 = jnp.min(jnp.where(logits == m1, lane, LANES), axis=-1, keepdims=True)
    rest = jnp.where(lane == i1, NEG_BIG, logits)
    m2 = jnp.max(rest, axis=-1, keepdims=True)
    i2 = jnp.min(jnp.where(rest == m2, lane, LANES), axis=-1, keepdims=True)
    e2 = jnp.exp(m2 - m1)
    g1 = 1.0 / (1.0 + e2)
    ids_ref[0, rows] = jnp.where(lane == 0, i1, jnp.where(lane == 1, i2, 0))
    gates_ref[0, rows] = jnp.where(lane == 0, g1, jnp.where(lane == 1, e2 * g1, 0.0))


def _route(x, oa, ob, oc, mod, lw, tm):
    b, t, d = x.shape

    def att(n):
        return pl.BlockSpec((1, n, tm, LANES), lambda i, j: (i, 0, j, 0))

    def const(shape):
        return pl.BlockSpec(shape, lambda i, j: (0,) * len(shape))

    tok = pl.BlockSpec((1, tm, d), lambda i, j: (i, j, 0))
    meta = pl.BlockSpec((1, tm, LANES), lambda i, j: (i, j, 0))
    return pl.pallas_call(
        _route_kernel,
        grid=(b, t // tm),
        in_specs=[tok, att(3), att(2), att(3), pl.BlockSpec((1, 6, d), lambda i, j: (i, 0, 0)),
                  const((d, d)), const((1, d)), const((d, LANES))],
        out_specs=[tok, tok, meta, meta],
        out_shape=[jax.ShapeDtypeStruct((b, t, d), F32), jax.ShapeDtypeStruct((b, t, d), F32),
                   jax.ShapeDtypeStruct((b, t, LANES), jnp.int32),
                   jax.ShapeDtypeStruct((b, t, LANES), F32)],
        compiler_params=_cparams(("parallel", "parallel")),
        name="moe_route",
    )(x, oa, ob, oc, mod, lw["w_out"], lw["g_ffn"], lw["w_router"])


def _row_copy(src_ref, src_row, dst_ref, dst_row, sem):
    return pltpu.make_async_copy(src_ref.at[pl.ds(src_row, 1)], dst_ref.at[pl.ds(dst_row, 1)], sem)


def _load_indices(idx_ref, tile, idx_smem, sem):
    cp = pltpu.make_async_copy(idx_ref.at[tile], idx_smem, sem)
    cp.start()
    cp.wait()


def _wait_rows(src_ref, dst_ref, sem):
    pltpu.make_async_copy(src_ref.at[pl.ds(0, dst_ref.shape[0])], dst_ref, sem).wait()


def _experts_kernel(te_ref, tv_ref, src_ref, hf_ref, w1_ref, w3_ref, w2_ref, y_ref,
                    src_smem, xg_scr, xb_scr, acc_scr, src_sem, sems):
    i, f = pl.program_id(0), pl.program_id(1)
    tm = xb_scr.shape[0]

    def index_copy(tile):
        return pltpu.make_async_copy(src_ref.at[tile], src_smem, src_sem)

    def issue_rows(tile):
        slot = tile % 2

        def issue(r, carry):
            _row_copy(hf_ref, src_smem[r], xg_scr.at[slot], r, sems.at[slot]).start()
            return carry

        lax.fori_loop(0, tm, issue, 0, unroll=ISSUE_UNROLL)

    nxt = jnp.minimum(i + 1, pl.num_programs(0) - 1)
    first_step = f == 0
    prefetch = first_step & (i + 1 < pl.num_programs(0)) & (tv_ref[nxt] > 0)

    @pl.when(first_step & (i == 0))
    def _():
        index_copy(0).start()
        index_copy(0).wait()
        issue_rows(0)

    @pl.when(prefetch)
    def _():
        index_copy(nxt).start()

    @pl.when(first_step & (tv_ref[i] > 0))
    def _():
        slot = i % 2
        _wait_rows(hf_ref, xg_scr.at[slot], sems.at[slot])
        xb_scr[...] = xg_scr[slot].astype(BF16)

    @pl.when(first_step)
    def _():
        acc_scr[...] = jnp.zeros(acc_scr.shape, F32)

    @pl.when(prefetch)
    def _():
        index_copy(nxt).wait()
        issue_rows(nxt)

    @pl.when(tv_ref[i] > 0)
    def _():
        xb = xb_scr[...]
        a = _dot(xb, w1_ref[...])
        gate = _dot(xb, w3_ref[...])
        acc_scr[...] += _dot((_silu(a) * gate).astype(BF16), w2_ref[...])

    @pl.when(f == pl.num_programs(1) - 1)
    def _():
        y_ref[...] = acc_scr[...]


def _experts(tile_expert, tile_valid, src, hf, lw, tf):
    n_tiles, tm = src.shape
    d = hf.shape[1]
    f = lw["w1"].shape[2]
    grid_spec = pltpu.PrefetchScalarGridSpec(
        num_scalar_prefetch=2,
        grid=(n_tiles, f // tf),
        in_specs=[pl.BlockSpec(memory_space=pltpu.VMEM), pl.BlockSpec(memory_space=pl.ANY),
                  pl.BlockSpec((None, d, tf), lambda i, k, te, tv: (te[i], 0, k)),
                  pl.BlockSpec((None, d, tf), lambda i, k, te, tv: (te[i], 0, k)),
                  pl.BlockSpec((None, tf, d), lambda i, k, te, tv: (te[i], k, 0))],
        out_specs=pl.BlockSpec((tm, d), lambda i, k, te, tv: (i, 0)),
        scratch_shapes=[pltpu.SMEM((tm,), jnp.int32), pltpu.VMEM((2, tm, d), F32),
                        pltpu.VMEM((tm, d), BF16), pltpu.VMEM((tm, d), F32),
                        pltpu.SemaphoreType.DMA(()), pltpu.SemaphoreType.DMA((2,))])
    return pl.pallas_call(
        _experts_kernel,
        grid_spec=grid_spec,
        out_shape=jax.ShapeDtypeStruct((n_tiles * tm, d), F32),
        compiler_params=_cparams(("arbitrary", "arbitrary")),
        name="moe_experts",
    )(tile_expert, tile_valid, src, hf, lw["w1"], lw["w3"], lw["w2"])


def _combine_kernel(pos_ref, y_ref, x1_ref, gates_ref, mod_ref, gfin_ref, o_ref,
                    pos_smem, y0_scr, y1_scr, pos_sem, sem):
    rows = y0_scr.shape[0]
    _load_indices(pos_ref, pl.program_id(0), pos_smem, pos_sem)

    def issue(r, carry):
        _row_copy(y_ref, pos_smem[r], y0_scr, r, sem).start()
        _row_copy(y_ref, pos_smem[rows + r], y1_scr, r, sem).start()
        return carry

    lax.fori_loop(0, rows, issue, 0, unroll=ISSUE_UNROLL)
    _wait_rows(y_ref, y0_scr, sem)
    _wait_rows(y_ref, y1_scr, sem)
    gates = gates_ref[...]
    y = gates[:, 0:1] * y0_scr[...] + gates[:, 1:2] * y1_scr[...]
    o_ref[...] = _rms(x1_ref[...] + mod_ref[0, 5:6] * y) * gfin_ref[...]


def _combine(pos, y, x1, gates, mod, gfin, tokens_per_batch):
    n, d = x1.shape
    n_tiles, two_rows = pos.shape
    rows = two_rows // 2
    tok = pl.BlockSpec((rows, d), lambda i: (i, 0))
    return pl.pallas_call(
        _combine_kernel,
        grid=(n_tiles,),
        in_specs=[pl.BlockSpec(memory_space=pltpu.VMEM), pl.BlockSpec(memory_space=pl.ANY), tok,
                  pl.BlockSpec((rows, LANES), lambda i: (i, 0)),
                  pl.BlockSpec((1, 6, d), lambda i: (i * rows // tokens_per_batch, 0, 0)),
                  pl.BlockSpec((1, d), lambda i: (0, 0))],
        out_specs=tok,
        out_shape=jax.ShapeDtypeStruct((n, d), F32),
        scratch_shapes=[pltpu.SMEM((two_rows,), jnp.int32), pltpu.VMEM((rows, d), F32),
                        pltpu.VMEM((rows, d), F32), pltpu.SemaphoreType.DMA(()),
                        pltpu.SemaphoreType.DMA(())],
        compiler_params=_cparams(("arbitrary",)),
        name="moe_combine",
    )(pos, y, x1, gates, mod, gfin)


def _dispatch_plan(ids, tm, rows):
    n = ids.shape[0]
    flat = ids.reshape(-1)
    onehot = (flat[None, :] == jnp.arange(N_EXPERTS)[:, None]).astype(jnp.int32)
    csum = jnp.cumsum(onehot, axis=1)
    rank = jnp.sum(csum * onehot, axis=0) - 1
    counts = csum[:, -1]
    padded = (counts + tm - 1) // tm * tm
    ends = jnp.cumsum(padded)
    pos = (ends - padded)[flat] + rank
    n_sorted = 2 * n + N_EXPERTS * tm
    starts = jnp.arange(n_sorted // tm, dtype=jnp.int32) * tm
    tile_expert = jnp.sum((starts[:, None] >= ends[None, :]).astype(jnp.int32), axis=1)
    tile_valid = (starts < ends[-1]).astype(jnp.int32)
    last_expert = tile_expert[jnp.maximum(ends[-1] // tm - 1, 0)]
    tile_expert = jnp.where(tile_valid > 0, tile_expert, last_expert)
    order = jnp.argsort(flat, stable=True).astype(jnp.int32)
    first = jnp.cumsum(counts) - counts
    row = jnp.arange(n_sorted, dtype=jnp.int32)
    row_expert = jnp.repeat(tile_expert, tm)
    r = row - (ends - padded)[row_expert]
    live = (r < counts[row_expert]) & (jnp.repeat(tile_valid, tm) > 0)
    src = jnp.where(live, order[jnp.clip(first[row_expert] + r, 0, 2 * n - 1)] // 2, 0)
    pos = pos.reshape(n // rows, rows, 2).transpose(0, 2, 1).reshape(n // rows, 2 * rows)
    return pos.astype(jnp.int32), src.reshape(n_sorted // tm, tm), tile_expert, tile_valid


def _ffn_moe(x, oa, ob, oc, mod, lw, gfin):
    b, t, d = x.shape
    n = b * t
    assert t % ROW_TILE == 0
    x1, hf, ids, gates = _route(x, oa, ob, oc, mod, lw, 512)
    pos, src, tile_expert, tile_valid = _dispatch_plan(ids.reshape(n, LANES)[:, :2], MOE_TM, ROW_TILE)
    y = _experts(tile_expert, tile_valid, src, hf.reshape(n, d), lw, MOE_TF)
    out = _combine(pos, y, x1.reshape(n, d), gates.reshape(n, LANES), mod, gfin, t)
    return out.reshape(b, t, d)


def _rope_tables(n_lat):
    rows = n_lat // GRID_W
    row = jnp.repeat(jnp.arange(rows, dtype=F32), GRID_W)
    col = jnp.tile(jnp.arange(GRID_W, dtype=F32), rows)

    def table(rot_dim):
        nf = rot_dim // 4
        inv = ROPE_THETA ** (-jnp.arange(nf, dtype=F32) / nf)
        ang_r, ang_c = row[:, None] * inv, col[:, None] * inv
        ang = jnp.concatenate([ang_r, ang_r, ang_c, ang_c], axis=-1)
        sign = jnp.tile(jnp.concatenate([-jnp.ones(nf, F32), jnp.ones(nf, F32)]), 2)
        return jnp.cos(ang), jnp.sin(ang) * sign

    cos_h, sin_h = table(HEAD_DIM)
    cos_r, sin_r = table(B_ROPE)
    one = jnp.ones((n_lat, B_NOPE), F32)
    pad = LANES - B_NOPE - B_ROPE
    cos_b = jnp.concatenate([one, cos_r, jnp.ones((n_lat, pad), F32)], axis=-1)
    sin_b = jnp.concatenate([0 * one, sin_r, jnp.zeros((n_lat, pad), F32)], axis=-1)
    return (jnp.tile(cos_h, (1, 2)), jnp.tile(sin_h, (1, 2)), cos_b, sin_b)


def _permute_heads(w, axis):
    shape = w.shape
    w = w.reshape(shape[:axis] + (6, HEAD_DIM) + shape[axis + 1:])
    w = jnp.take(w, jnp.array(HEAD_PERM), axis=axis)
    return w.reshape(shape)


def _layer_weights(i, w_in, norm_ffn, a_q_norm, a_k_norm, b_q_norm, b_kv_norm, w_uq, w_ukv, w_out):
    d = w_in.shape[1]
    wi = w_in[i]
    qk = A_HEADS * HEAD_DIM
    kv = A_KV * HEAD_DIM
    b0 = A_COLS
    c0 = A_COLS + B_COLS
    k_rope = jnp.pad(wi[:, b0 + B_Q_RANK + B_KV_RANK:c0], ((0, 0), (B_NOPE, LANES - B_NOPE - B_ROPE)))
    w_in_p = jnp.concatenate([
        _permute_heads(wi[:, :qk], 1), wi[:, qk:A_COLS],
        wi[:, b0:b0 + B_Q_RANK + B_KV_RANK], k_rope,
        _permute_heads(wi[:, c0:c0 + qk], 1), wi[:, c0 + qk:]], axis=1).astype(BF16)
    w_uq_p = jnp.pad(w_uq[i].reshape(B_Q_RANK, B_HEADS, B_NOPE + B_ROPE),
                     ((0, 0), (0, 0), (0, LANES - B_NOPE - B_ROPE))).reshape(B_Q_RANK, B_HEADS * LANES)
    ukv = w_ukv[i].reshape(B_KV_RANK, B_HEADS, B_NOPE + B_V)
    k_up = jnp.pad(ukv[:, :, :B_NOPE], ((0, 0), (0, 0), (0, LANES - B_NOPE))).reshape(B_KV_RANK, -1)
    v_up = ukv[:, :, B_NOPE:].reshape(B_KV_RANK, -1)
    wo = w_out[i]
    w_out_p = jnp.concatenate([_permute_heads(wo[:qk], 0), wo[qk:qk + B_HEADS * B_V],
                               _permute_heads(wo[qk + B_HEADS * B_V:], 0)], axis=0).astype(BF16)
    blk = (jnp.arange(LANES)[:, None] // HEAD_DIM) == (jnp.arange(LANES)[None, :] // HEAD_DIM)
    del kv, d
    return {
        "w_in": w_in_p,
        "m2": (blk.astype(F32) / HEAD_DIM).astype(BF16),
        "a_q_gain": jnp.tile(a_q_norm[i], 2)[None], "a_k_gain": jnp.tile(a_k_norm[i], 2)[None],
        "b_q_gain": b_q_norm[i][None], "b_kv_gain": b_kv_norm[i][None],
        "w_uq": w_uq_p.astype(BF16),
        "w_ukv": jnp.concatenate([k_up, v_up], axis=1).astype(BF16),
        "w_out": w_out_p,
        "g_ffn": norm_ffn[i][None],
    }


def kernel(x, c, ctx, c_ctx, w_mod, b_mod, norm_mix, norm_ffn, w_in, a_q_norm, a_k_norm, b_q_norm,
           b_kv_norm, w_uq, w_ukv, c_sinks, w_out, w1_dense, w3_dense, w2_dense, w_router, w1_moe,
           w3_moe, w2_moe, final_norm):
    bsz, n_lat, d = x.shape
    n_ctx = ctx.shape[1]
    depth = w_mod.shape[0]
    mod_rows = 16
    cc = jnp.concatenate([c, c_ctx[None], jnp.zeros((mod_rows - bsz - 1, d), F32)], axis=0)
    mods = _modulation(cc, w_mod, b_mod)
    tables = _rope_tables(n_lat)
    no_sink = jnp.zeros((A_HEADS,), F32)
    xc = ctx
    for i in range(depth):
        last = i == depth - 1
        mod_lat = mods[i, :bsz].reshape(bsz, 6, d)
        mod_ctx = mods[i, bsz:bsz + 1].reshape(1, 6, d)
        lw = _layer_weights(i, w_in, norm_ffn, a_q_norm, a_k_norm, b_q_norm, b_kv_norm, w_uq, w_ukv,
                            w_out)
        g_mix = norm_mix[i][None]
        lat = _in_projection(x, mod_lat, True, g_mix, lw, tables, True, 512)
        cx = _in_projection(xc, mod_ctx, False, g_mix, lw, tables, False, n_ctx)
        qa, ka, va, qb, kb, vb, qc, kc, vc = lat
        cqa, cka, cva, cqb, ckb, cvb, cqc, ckc, cvc = cx

        def both(u, w):
            return jnp.concatenate([u, w], axis=2)

        sinks = jnp.take(c_sinks[i], jnp.array(HEAD_PERM))
        oa = _flash(qa, both(ka, cka), both(va, cva), no_sink, (0,) * 6, (0,) * 6, False, 512)
        ob = _flash(qb, both(kb, ckb), both(vb, cvb), no_sink, (0, 1, 2, 3), (0, 0, 1, 1), False, 512)
        oc = _window_attention(qc, both(kc, ckc), both(vc, cvc), sinks, n_lat)
        if not last:
            coa = _flash(cqa, cka, cva, no_sink, (0,) * 6, (0,) * 6, False, n_ctx)
            cob = _flash(cqb, ckb, cvb, no_sink, (0, 1, 2, 3), (0, 0, 1, 1), False, n_ctx)
            coc = _flash(cqc, ckc, cvc, sinks, (0,) * 6, (0,) * 6, True, n_ctx)
        j = i // 2
        if i % 2 == 0:
            lw.update(w1=w1_dense[j].astype(BF16), w3=w3_dense[j].astype(BF16),
                      w2=w2_dense[j].astype(BF16))
            x_new = _ffn_dense(x, oa, ob, oc, mod_lat, True, lw, 512)
            if not last:
                xc = _ffn_dense(xc, coa, cob, coc, mod_ctx, False, lw, n_ctx)
            x = x_new
        else:
            assert last, "the expert channel mixer is fused with the final norm"
            lw.update(w1=w1_moe[j].astype(BF16), w3=w3_moe[j].astype(BF16), w2=w2_moe[j].astype(BF16),
                      w_router=jnp.pad(w_router[j], ((0, 0), (0, LANES - N_EXPERTS))))
            x = _ffn_moe(x, oa, ob, oc, mod_lat, lw, final_norm[None])
    return x
```

```python
import functools

import jax
import jax.numpy as jnp
from jax import lax
from jax.experimental import pallas as pl
from jax.experimental.pallas import tpu as pltpu

F32 = jnp.float32
BF16 = jnp.bfloat16

LANES = 128
HEAD_DIM = 64
GRID_W = 64
ROPE_THETA = 10000.0
EPS = 1e-6
NEG_BIG = -1e30
A_HEADS, A_KV = 6, 2
B_HEADS, B_Q_RANK, B_KV_RANK, B_NOPE, B_ROPE, B_V = 4, 256, 128, 64, 32, 64
C_HEADS, C_KV, C_WINDOW = 6, 2, 128
N_EXPERTS = 8
A_COLS = (A_HEADS + 2 * A_KV) * HEAD_DIM
B_COLS = B_Q_RANK + B_KV_RANK + B_ROPE
VMEM_LIMIT = 56 * 1024 * 1024
LOG2E = 1.4426950408889634
FLASH_TK = 2304
MXU_COLS = 256
WINDOWS_PER_STEP = 8
ISSUE_UNROLL = 8
MOE_TM = 1024
MOE_TF = 512
ROW_TILE = 1024

HEAD_PERM = tuple((j % 2) * 3 + j // 2 for j in range(6))


def _cparams(sem):
    return pltpu.CompilerParams(dimension_semantics=sem, vmem_limit_bytes=VMEM_LIMIT)


def _rms(x):
    return x * lax.rsqrt(jnp.mean(x * x, axis=-1, keepdims=True) + EPS)


def _dot(a, b):
    return jnp.dot(a, b, preferred_element_type=F32)


def _dot_nt(a, b):
    return lax.dot_general(a, b, (((1,), (1,)), ((), ())), preferred_element_type=F32)


def _split(a):
    hi = a.astype(BF16)
    return hi, (a - hi.astype(F32)).astype(BF16)


def _dot3(a, b):
    ah, al = _split(a)
    bh, bl = _split(b)
    return _dot(ah, bh) + (_dot(ah, bl) + _dot(al, bh))


def _silu(a):
    return a * jax.nn.sigmoid(a)


def _lane(shape):
    return lax.broadcasted_iota(jnp.int32, shape, len(shape) - 1)


def _mod_kernel(c_ref, w_ref, b_ref, o_ref):
    o_ref[...] = _dot3(_silu(c_ref[...]), w_ref[...]) + b_ref[...]


def _modulation(cc, w_mod, b_mod):
    depth, d, n = w_mod.shape
    rows = cc.shape[0]
    tn = 1536
    return pl.pallas_call(
        _mod_kernel,
        grid=(depth, n // tn),
        in_specs=[pl.BlockSpec((rows, d), lambda i, j: (0, 0)),
                  pl.BlockSpec((None, d, tn), lambda i, j: (i, 0, j)),
                  pl.BlockSpec((None, 1, tn), lambda i, j: (i, 0, j))],
        out_specs=pl.BlockSpec((None, rows, tn), lambda i, j: (i, 0, j)),
        out_shape=jax.ShapeDtypeStruct((depth, rows, n), F32),
        compiler_params=_cparams(("arbitrary", "arbitrary")),
        name="modulation",
    )(cc, w_mod, b_mod.reshape(depth, 1, n))


def _rope(t, cos, sin_signed, half):
    first = (_lane(t.shape) & (2 * half - 1)) < half
    rot = jnp.where(first, pltpu.roll(t, LANES - half, 1), pltpu.roll(t, half, 1))
    return t * cos + rot * sin_signed


def _store_pair(ref, p, rows, t):
    left = _lane(t.shape) < HEAD_DIM
    ref[0, 2 * p, rows] = jnp.where(left, t, 0.0).astype(BF16)
    ref[0, 2 * p + 1, rows] = jnp.where(left, 0.0, t).astype(BF16)


def _proj_kernel(*refs, use_rope, n_parts):
    part_rows = refs[0].shape[1] // n_parts
    for part in range(n_parts):
        _proj_rows(slice(part * part_rows, (part + 1) * part_rows), use_rope, *refs)


def _proj_rows(rows, use_rope, x_ref, mod_ref, g_ref, win_ref, m2_ref, gaq_ref, gak_ref, gbq_ref,
               gbkv_ref, wuq_ref, wukv_ref, cosa_ref, sina_ref, cosb_ref, sinb_ref,
               qa_ref, ka_ref, va_ref, qb_ref, kb_ref, vb_ref, qc_ref, kc_ref, vc_ref):
    mod = mod_ref[0]
    h = _rms(x_ref[0, rows]) * g_ref[...]
    h = h * (1.0 + mod[1:2]) + mod[0:1]
    proj = _dot(h.astype(BF16), win_ref[...])

    def chunk(i):
        return proj[:, i * LANES:(i + 1) * LANES]

    def head_norm(t, gain):
        ms = _dot((t * t).astype(BF16), m2_ref[...])
        return t * lax.rsqrt(ms + EPS) * gain

    def rope_a(t):
        return _rope(t, cosa_ref[rows], sina_ref[rows], HEAD_DIM // 4) if use_rope else t

    def rope_b(t):
        return _rope(t, cosb_ref[rows], sinb_ref[rows], B_ROPE // 4) if use_rope else t

    scale = HEAD_DIM ** -0.5 * LOG2E
    for p in range(3):
        _store_pair(qa_ref, p, rows, rope_a(head_norm(chunk(p), gaq_ref[...])) * scale)
    ka_ref[0, 0, rows] = rope_a(head_norm(chunk(3), gak_ref[...])).astype(BF16)
    va_ref[0, 0, rows] = chunk(4).astype(BF16)
    cq = _rms(proj[:, 5 * LANES:7 * LANES]) * gbq_ref[...]
    qb = _dot(cq.astype(BF16), wuq_ref[...])
    ckv = _rms(chunk(7)) * gbkv_ref[...]
    kvb = _dot(ckv.astype(BF16), wukv_ref[...])
    k_rope = rope_b(chunk(8))
    scale_b = (B_NOPE + B_ROPE) ** -0.5 * LOG2E
    for hd in range(B_HEADS):
        qb_ref[0, hd, rows] = (rope_b(qb[:, hd * LANES:(hd + 1) * LANES]) * scale_b).astype(BF16)
        kb_ref[0, hd, rows] = (kvb[:, hd * LANES:(hd + 1) * LANES] + k_rope).astype(BF16)
    for p in range(2):
        vb_ref[0, p, rows] = kvb[:, (B_HEADS + p) * LANES:(B_HEADS + p + 1) * LANES].astype(BF16)
    for p in range(3):
        _store_pair(qc_ref, p, rows, rope_a(chunk(9 + p)) * scale)
    kc_ref[0, 0, rows] = rope_a(chunk(12)).astype(BF16)
    vc_ref[0, 0, rows] = chunk(13).astype(BF16)


def _in_projection(x, mod, per_batch_mod, g, lw, tables, use_rope, tm):
    b, t, d = x.shape
    ncol = lw["w_in"].shape[1]
    heads = (6, 1, 1, 4, 4, 2, 6, 1, 1)

    def const(shape):
        return pl.BlockSpec(shape, lambda i, j: (0,) * len(shape))

    mod_map = (lambda i, j: (i, 0, 0)) if per_batch_mod else (lambda i, j: (0, 0, 0))
    tab = pl.BlockSpec((tm, LANES), lambda i, j: (j, 0))
    return pl.pallas_call(
        functools.partial(_proj_kernel, use_rope=use_rope, n_parts=2),
        grid=(b, t // tm),
        in_specs=[pl.BlockSpec((1, tm, d), lambda i, j: (i, j, 0)),
                  pl.BlockSpec((1, 6, d), mod_map),
                  const((1, d)), const((d, ncol)), const((LANES, LANES)),
                  const((1, LANES)), const((1, LANES)), const((1, B_Q_RANK)), const((1, B_KV_RANK)),
                  const((B_Q_RANK, 4 * LANES)), const((B_KV_RANK, 6 * LANES)),
                  tab, tab, tab, tab],
        out_specs=[pl.BlockSpec((1, n, tm, LANES), lambda i, j: (i, 0, j, 0)) for n in heads],
        out_shape=[jax.ShapeDtypeStruct((b, n, t, LANES), BF16) for n in heads],
        compiler_params=_cparams(("parallel", "parallel")),
        name="in_projection",
    )(x, mod, g, lw["w_in"], lw["m2"], lw["a_q_gain"], lw["a_k_gain"], lw["b_q_gain"],
      lw["b_kv_gain"], lw["w_uq"], lw["w_ukv"], *tables)


def _tile_chunks(t_keys, max_chunk):
    tiles = -(-t_keys // MXU_COLS)
    n = -(-tiles * MXU_COLS // max_chunk)
    sizes = [(tiles // n + (c < tiles % n)) * MXU_COLS for c in range(n)]
    sizes[-1] -= tiles * MXU_COLS - t_keys
    return tuple(sizes)


def _flash_kernel(sink_ref, q_ref, k_ref, v_ref, o_ref, *, kidx, vidx, tk, use_sink):
    nh, tq = q_ref.shape[1], q_ref.shape[2]
    t_keys = k_ref.shape[2]
    chunks = _tile_chunks(t_keys, tk)
    left = _lane((tq, LANES)) < HEAD_DIM
    for pair in range(nh // 2):
        heads = (2 * pair, 2 * pair + 1)

        def step(start, size, carry, heads=heads):
            new = []
            for j, (m, l, acc) in zip(heads, carry):
                k = k_ref[0, kidx[j], pl.ds(start, size), :]
                v = v_ref[0, vidx[j], pl.ds(start, size), :]
                s = _dot_nt(q_ref[0, j], k)
                m_new = jnp.maximum(m, jnp.max(s, axis=-1, keepdims=True))
                alpha = jnp.exp2(m - m_new)
                p = jnp.exp2(s - m_new)
                l = alpha * l + jnp.sum(p, axis=-1, keepdims=True)
                acc = alpha * acc + _dot(p.astype(BF16), v)
                new.append((m_new, l, acc))
            return tuple(new)

        def init(j):
            if use_sink:
                m, l = jnp.full((tq, 1), sink_ref[j] * LOG2E, F32), jnp.ones((tq, 1), F32)
            else:
                m, l = jnp.full((tq, 1), NEG_BIG, F32), jnp.zeros((tq, 1), F32)
            return m, l, jnp.zeros((tq, LANES), F32)

        carry = tuple(init(j) for j in heads)
        start = 0
        for size in chunks:
            carry = step(start, size, carry)
            start += size
        even, odd = (c[2] / c[1] for c in carry)
        o_ref[0, pair] = jnp.where(left, even, odd).astype(BF16)


def _flash(q, k, v, sinks, kidx, vidx, use_sink, tq):
    b, nh, t, _ = q.shape
    nk, tkeys = k.shape[1], k.shape[2]
    nv = v.shape[1]
    return pl.pallas_call(
        functools.partial(_flash_kernel, kidx=kidx, vidx=vidx, tk=FLASH_TK, use_sink=use_sink),
        grid=(b, t // tq),
        in_specs=[pl.BlockSpec(memory_space=pltpu.SMEM),
                  pl.BlockSpec((1, nh, tq, LANES), lambda i, j: (i, 0, j, 0)),
                  pl.BlockSpec((1, nk, tkeys, LANES), lambda i, j: (i, 0, 0, 0)),
                  pl.BlockSpec((1, nv, tkeys, LANES), lambda i, j: (i, 0, 0, 0))],
        out_specs=pl.BlockSpec((1, nh // 2, tq, LANES), lambda i, j: (i, 0, j, 0)),
        out_shape=jax.ShapeDtypeStruct((b, nh // 2, t, LANES), BF16),
        compiler_params=_cparams(("parallel", "parallel")),
        name="flash_attention",
    )(sinks, q, k, v)


def _window_block(sink, q, ks, vs, has_prev, has_next, w):
    k = jnp.concatenate(ks, axis=0)
    v = jnp.concatenate(vs, axis=0)
    s = _dot_nt(q, k)
    qq = lax.broadcasted_iota(jnp.int32, s.shape, 0) & (w - 1)
    col = lax.broadcasted_iota(jnp.int32, s.shape, 1)
    never = 4 * w
    off_prev = jnp.where(has_prev, 0, never)
    off_next = jnp.where(has_next, 0, never)
    valid = ((col < w) & (col >= qq + off_prev)) | ((col >= w) & (col < 2 * w)) \
        | ((col >= 2 * w) & (col < 3 * w) & (col - 2 * w + off_next <= qq)) | (col >= 3 * w)
    s = jnp.where(valid, s, NEG_BIG)
    m = jnp.maximum(sink, jnp.max(s, axis=-1, keepdims=True))
    e = jnp.exp2(s - m)
    denom = jnp.exp2(sink - m) + jnp.sum(e, axis=-1, keepdims=True)
    return _dot(e.astype(BF16), v) / denom


def _window_kernel(sink_ref, q_ref, kp_ref, kc_ref, kn_ref, kx_ref, vp_ref, vc_ref, vn_ref, vx_ref,
                   o_ref):
    nh, w = q_ref.shape[1], kp_ref.shape[2]
    n_win = q_ref.shape[2] // w
    i, steps = pl.program_id(1), pl.num_programs(1)
    sink = jnp.concatenate([jnp.full((w, 1), sink_ref[j] * LOG2E, F32) for j in range(nh)], axis=0)
    ks = [kp_ref[0, 0]] + [kc_ref[0, 0, c * w:(c + 1) * w] for c in range(n_win)] + [kn_ref[0, 0]]
    vs = [vp_ref[0, 0]] + [vc_ref[0, 0, c * w:(c + 1) * w] for c in range(n_win)] + [vn_ref[0, 0]]
    kx, vx = kx_ref[0, 0], vx_ref[0, 0]
    left = _lane((w, LANES)) < HEAD_DIM
    for c in range(n_win):
        rows = slice(c * w, (c + 1) * w)
        has_prev = (i >= 1) if c == 0 else True
        has_next = (i <= steps - 2) if c == n_win - 1 else True
        o = _window_block(sink, q_ref[0, :, rows].reshape(nh * w, LANES), ks[c:c + 3] + [kx],
                          vs[c:c + 3] + [vx], has_prev, has_next, w)
        for p in range(nh // 2):
            o_ref[0, p, rows] = jnp.where(left, o[2 * p * w:(2 * p + 1) * w],
                                          o[(2 * p + 1) * w:(2 * p + 2) * w]).astype(BF16)


def _window_attention(q, k_all, v_all, sinks, n_lat):
    b, nh, _, _ = q.shape
    w = C_WINDOW
    nb = n_lat // w
    n_ctx = k_all.shape[2] - n_lat
    g = max(c for c in range(1, WINDOWS_PER_STEP + 1) if nb % c == 0)
    assert n_lat % n_ctx == 0 and w & (w - 1) == 0
    prev = pl.BlockSpec((1, 1, w, LANES), lambda i, j: (i, 0, jnp.maximum(g * j - 1, 0), 0))
    cur = pl.BlockSpec((1, 1, g * w, LANES), lambda i, j: (i, 0, j, 0))
    nxt = pl.BlockSpec((1, 1, w, LANES), lambda i, j: (i, 0, jnp.minimum(g * j + g, nb - 1), 0))
    ctx = pl.BlockSpec((1, 1, n_ctx, LANES), lambda i, j: (i, 0, n_lat // n_ctx, 0))
    return pl.pallas_call(
        _window_kernel,
        grid=(b, nb // g),
        in_specs=[pl.BlockSpec(memory_space=pltpu.SMEM),
                  pl.BlockSpec((1, nh, g * w, LANES), lambda i, j: (i, 0, j, 0)),
                  prev, cur, nxt, ctx, prev, cur, nxt, ctx],
        out_specs=pl.BlockSpec((1, nh // 2, g * w, LANES), lambda i, j: (i, 0, j, 0)),
        out_shape=jax.ShapeDtypeStruct((b, nh // 2, n_lat, LANES), BF16),
        compiler_params=_cparams(("parallel", "parallel")),
        name="window_attention",
    )(sinks, q, k_all, k_all, k_all, k_all, v_all, v_all, v_all, v_all)


def _mixed_residual(x_ref, oa_ref, ob_ref, oc_ref, mod, wout_ref, rows=slice(None)):
    att = jnp.concatenate([oa_ref[0, 0, rows], oa_ref[0, 1, rows], oa_ref[0, 2, rows],
                           ob_ref[0, 0, rows], ob_ref[0, 1, rows],
                           oc_ref[0, 0, rows], oc_ref[0, 1, rows], oc_ref[0, 2, rows]], axis=-1)
    return x_ref[0, rows] + mod[2:3] * _dot(att, wout_ref[...])


def _ffn_dense_kernel(x_ref, oa_ref, ob_ref, oc_ref, mod_ref, wout_ref, g_ref, w1_ref, w3_ref, w2_ref,
                      o_ref, *, n_chunks):
    mod = mod_ref[0]
    x1 = _mixed_residual(x_ref, oa_ref, ob_ref, oc_ref, mod, wout_ref)
    hf = ((_rms(x1) * g_ref[...]) * (1.0 + mod[4:5]) + mod[3:4]).astype(BF16)
    y = jnp.zeros(x1.shape, F32)
    start = 0
    for size in _tile_chunks(w1_ref.shape[1], -(-w1_ref.shape[1] // n_chunks)):
        cols = slice(start, start + size)
        a = _dot(hf, w1_ref[:, cols])
        gate = _dot(hf, w3_ref[:, cols])
        y = y + _dot((_silu(a) * gate).astype(BF16), w2_ref[cols, :])
        start += size
    o_ref[0] = x1 + mod[5:6] * y


def _resident(shape):
    return pl.BlockSpec(shape, lambda i, j: (0,) * len(shape), pipeline_mode=pl.Buffered(1))


def _ffn_dense(x, oa, ob, oc, mod, per_batch_mod, lw, tm):
    b, t, d = x.shape
    f = lw["w1"].shape[1]
    mod_map = (lambda i, j: (i, 0, 0)) if per_batch_mod else (lambda i, j: (0, 0, 0))

    def att(n):
        return pl.BlockSpec((1, n, tm, LANES), lambda i, j: (i, 0, j, 0))

    return pl.pallas_call(
        functools.partial(_ffn_dense_kernel, n_chunks=2),
        grid=(b, t // tm),
        in_specs=[pl.BlockSpec((1, tm, d), lambda i, j: (i, j, 0)), att(3), att(2), att(3),
                  pl.BlockSpec((1, 6, d), mod_map),
                  _resident((d, d)), _resident((1, d)),
                  _resident((d, f)), _resident((d, f)), _resident((f, d))],
        out_specs=pl.BlockSpec((1, tm, d), lambda i, j: (i, j, 0)),
        out_shape=jax.ShapeDtypeStruct((b, t, d), F32),
        compiler_params=_cparams(("parallel", "parallel")),
        name="ffn_dense",
    )(x, oa, ob, oc, mod, lw["w_out"], lw["g_ffn"], lw["w1"], lw["w3"], lw["w2"])


def _route_kernel(x_ref, oa_ref, ob_ref, oc_ref, mod_ref, wout_ref, g_ref, wr_ref,
                  x1_ref, hf_ref, ids_ref, gates_ref):
    half = x_ref.shape[1] // 2
    for part in range(2):
        _route_rows(slice(part * half, (part + 1) * half), x_ref, oa_ref, ob_ref, oc_ref, mod_ref,
                    wout_ref, g_ref, wr_ref, x1_ref, hf_ref, ids_ref, gates_ref)


def _route_rows(rows, x_ref, oa_ref, ob_ref, oc_ref, mod_ref, wout_ref, g_ref, wr_ref,
                x1_ref, hf_ref, ids_ref, gates_ref):
    mod = mod_ref[0]
    x1 = _mixed_residual(x_ref, oa_ref, ob_ref, oc_ref, mod, wout_ref, rows)
    hf = (_rms(x1) * g_ref[...]) * (1.0 + mod[4:5]) + mod[3:4]
    x1_ref[0, rows] = x1
    hf_ref[0, rows] = hf
    logits = _dot3(hf, wr_ref[...])
    lane = _lane(logits.shape)
    logits = jnp.where(lane < N_EXPERTS, logits, NEG_BIG)
    m1 = jnp.max(logits, axis=-1, keepdims=True)
    i1 = jnp.min(jnp.where(logits == m1, lane, LANES), axis=-1, keepdims=True)
    rest = jnp.where(lane == i1, NEG_BIG, logits)
    m2 = jnp.max(rest, axis=-1, keepdims=True)
    i2 = jnp.min(jnp.where(rest == m2, lane, LANES), axis=-1, keepdims=True)
    e2 = jnp.exp(m2 - m1)
    g1 = 1.0 / (1.0 + e2)
    ids_ref[0, rows] = jnp.where(lane == 0, i1, jnp.where(lane == 1, i2, 0))
    gates_ref[0, rows] = jnp.where(lane == 0, g1, jnp.where(lane == 1, e2 * g1, 0.0))


def _route(x, oa, ob, oc, mod, lw, tm):
    b, t, d = x.shape

    def att(n):
        return pl.BlockSpec((1, n, tm, LANES), lambda i, j: (i, 0, j, 0))

    def const(shape):
        return pl.BlockSpec(shape, lambda i, j: (0,) * len(shape))

    tok = pl.BlockSpec((1, tm, d), lambda i, j: (i, j, 0))
    meta = pl.BlockSpec((1, tm, LANES), lambda i, j: (i, j, 0))
    return pl.pallas_call(
        _route_kernel,
        grid=(b, t // tm),
        in_specs=[tok, att(3), att(2), att(3), pl.BlockSpec((1, 6, d), lambda i, j: (i, 0, 0)),
                  const((d, d)), const((1, d)), const((d, LANES))],
        out_specs=[tok, tok, meta, meta],
        out_shape=[jax.ShapeDtypeStruct((b, t, d), F32), jax.ShapeDtypeStruct((b, t, d), F32),
                   jax.ShapeDtypeStruct((b, t, LANES), jnp.int32),
                   jax.ShapeDtypeStruct((b, t, LANES), F32)],
        compiler_params=_cparams(("parallel", "parallel")),
        name="moe_route",
    )(x, oa, ob, oc, mod, lw["w_out"], lw["g_ffn"], lw["w_router"])


def _row_copy(src_ref, src_row, dst_ref, dst_row, sem):
    return pltpu.make_async_copy(src_ref.at[pl.ds(src_row, 1)], dst_ref.at[pl.ds(dst_row, 1)], sem)


def _load_indices(idx_ref, tile, idx_smem, sem):
    cp = pltpu.make_async_copy(idx_ref.at[tile], idx_smem, sem)
    cp.start()
    cp.wait()


def _wait_rows(src_ref, dst_ref, sem):
    pltpu.make_async_copy(src_ref.at[pl.ds(0, dst_ref.shape[0])], dst_ref, sem).wait()


def _experts_kernel(te_ref, tv_ref, src_ref, hf_ref, w1_ref, w3_ref, w2_ref, y_ref,
                    src_smem, xg_scr, xb_scr, acc_scr, src_sem, sems):
    i, f = pl.program_id(0), pl.program_id(1)
    tm = xb_scr.shape[0]

    def index_copy(tile):
        return pltpu.make_async_copy(src_ref.at[tile], src_smem, src_sem)

    def issue_rows(tile):
        slot = tile % 2

        def issue(r, carry):
            _row_copy(hf_ref, src_smem[r], xg_scr.at[slot], r, sems.at[slot]).start()
            return carry

        lax.fori_loop(0, tm, issue, 0, unroll=ISSUE_UNROLL)

    nxt = jnp.minimum(i + 1, pl.num_programs(0) - 1)
    first_step = f == 0
    prefetch = first_step & (i + 1 < pl.num_programs(0)) & (tv_ref[nxt] > 0)

    @pl.when(first_step & (i == 0))
    def _():
        index_copy(0).start()
        index_copy(0).wait()
        issue_rows(0)

    @pl.when(prefetch)
    def _():
        index_copy(nxt).start()

    @pl.when(first_step & (tv_ref[i] > 0))
    def _():
        slot = i % 2
        _wait_rows(hf_ref, xg_scr.at[slot], sems.at[slot])
        xb_scr[...] = xg_scr[slot].astype(BF16)

    @pl.when(first_step)
    def _():
        acc_scr[...] = jnp.zeros(acc_scr.shape, F32)

    @pl.when(prefetch)
    def _():
        index_copy(nxt).wait()
        issue_rows(nxt)

    @pl.when(tv_ref[i] > 0)
    def _():
        xb = xb_scr[...]
        a = _dot(xb, w1_ref[...])
        gate = _dot(xb, w3_ref[...])
        acc_scr[...] += _dot((_silu(a) * gate).astype(BF16), w2_ref[...])

    @pl.when(f == pl.num_programs(1) - 1)
    def _():
        y_ref[...] = acc_scr[...]


def _experts(tile_expert, tile_valid, src, hf, lw, tf):
    n_tiles, tm = src.shape
    d = hf.shape[1]
    f = lw["w1"].shape[2]
    grid_spec = pltpu.PrefetchScalarGridSpec(
        num_scalar_prefetch=2,
        grid=(n_tiles, f // tf),
        in_specs=[pl.BlockSpec(memory_space=pltpu.VMEM), pl.BlockSpec(memory_space=pl.ANY),
                  pl.BlockSpec((None, d, tf), lambda i, k, te, tv: (te[i], 0, k)),
                  pl.BlockSpec((None, d, tf), lambda i, k, te, tv: (te[i], 0, k)),
                  pl.BlockSpec((None, tf, d), lambda i, k, te, tv: (te[i], k, 0))],
        out_specs=pl.BlockSpec((tm, d), lambda i, k, te, tv: (i, 0)),
        scratch_shapes=[pltpu.SMEM((tm,), jnp.int32), pltpu.VMEM((2, tm, d), F32),
                        pltpu.VMEM((tm, d), BF16), pltpu.VMEM((tm, d), F32),
                        pltpu.SemaphoreType.DMA(()), pltpu.SemaphoreType.DMA((2,))])
    return pl.pallas_call(
        _experts_kernel,
        grid_spec=grid_spec,
        out_shape=jax.ShapeDtypeStruct((n_tiles * tm, d), F32),
        compiler_params=_cparams(("arbitrary", "arbitrary")),
        name="moe_experts",
    )(tile_expert, tile_valid, src, hf, lw["w1"], lw["w3"], lw["w2"])


def _combine_kernel(pos_ref, y_ref, x1_ref, gates_ref, mod_ref, gfin_ref, o_ref,
                    pos_smem, y0_scr, y1_scr, pos_sem, sem):
    rows = y0_scr.shape[0]
    _load_indices(pos_ref, pl.program_id(0), pos_smem, pos_sem)

    def issue(r, carry):
        _row_copy(y_ref, pos_smem[r], y0_scr, r, sem).start()
        _row_copy(y_ref, pos_smem[rows + r], y1_scr, r, sem).start()
        return carry

    lax.fori_loop(0, rows, issue, 0, unroll=ISSUE_UNROLL)
    _wait_rows(y_ref, y0_scr, sem)
    _wait_rows(y_ref, y1_scr, sem)
    gates = gates_ref[...]
    y = gates[:, 0:1] * y0_scr[...] + gates[:, 1:2] * y1_scr[...]
    o_ref[...] = _rms(x1_ref[...] + mod_ref[0, 5:6] * y) * gfin_ref[...]


def _combine(pos, y, x1, gates, mod, gfin, tokens_per_batch):
    n, d = x1.shape
    n_tiles, two_rows = pos.shape
    rows = two_rows // 2
    tok = pl.BlockSpec((rows, d), lambda i: (i, 0))
    return pl.pallas_call(
        _combine_kernel,
        grid=(n_tiles,),
        in_specs=[pl.BlockSpec(memory_space=pltpu.VMEM), pl.BlockSpec(memory_space=pl.ANY), tok,
                  pl.BlockSpec((rows, LANES), lambda i: (i, 0)),
                  pl.BlockSpec((1, 6, d), lambda i: (i * rows // tokens_per_batch, 0, 0)),
                  pl.BlockSpec((1, d), lambda i: (0, 0))],
        out_specs=tok,
        out_shape=jax.ShapeDtypeStruct((n, d), F32),
        scratch_shapes=[pltpu.SMEM((two_rows,), jnp.int32), pltpu.VMEM((rows, d), F32),
                        pltpu.VMEM((rows, d), F32), pltpu.SemaphoreType.DMA(()),
                        pltpu.SemaphoreType.DMA(())],
        compiler_params=_cparams(("arbitrary",)),
        name="moe_combine",
    )(pos, y, x1, gates, mod, gfin)


def _dispatch_plan(ids, tm, rows):
    n = ids.shape[0]
    flat = ids.reshape(-1)
    onehot = (flat[None, :] == jnp.arange(N_EXPERTS)[:, None]).astype(jnp.int32)
    csum = jnp.cumsum(onehot, axis=1)
    rank = jnp.sum(csum * onehot, axis=0) - 1
    counts = csum[:, -1]
    padded = (counts + tm - 1) // tm * tm
    ends = jnp.cumsum(padded)
    pos = (ends - padded)[flat] + rank
    n_sorted = 2 * n + N_EXPERTS * tm
    starts = jnp.arange(n_sorted // tm, dtype=jnp.int32) * tm
    tile_expert = jnp.sum((starts[:, None] >= ends[None, :]).astype(jnp.int32), axis=1)
    tile_valid = (starts < ends[-1]).astype(jnp.int32)
    last_expert = tile_expert[jnp.maximum(ends[-1] // tm - 1, 0)]
    tile_expert = jnp.where(tile_valid > 0, tile_expert, last_expert)
    order = jnp.argsort(flat, stable=True).astype(jnp.int32)
    first = jnp.cumsum(counts) - counts
    row = jnp.arange(n_sorted, dtype=jnp.int32)
    row_expert = jnp.repeat(tile_expert, tm)
    r = row - (ends - padded)[row_expert]
    live = (r < counts[row_expert]) & (jnp.repeat(tile_valid, tm) > 0)
    src = jnp.where(live, order[jnp.clip(first[row_expert] + r, 0, 2 * n - 1)] // 2, 0)
    pos = pos.reshape(n // rows, rows, 2).transpose(0, 2, 1).reshape(n // rows, 2 * rows)
    return pos.astype(jnp.int32), src.reshape(n_sorted // tm, tm), tile_expert, tile_valid


def _ffn_moe(x, oa, ob, oc, mod, lw, gfin):
    b, t, d = x.shape
    n = b * t
    assert t % ROW_TILE == 0
    x1, hf, ids, gates = _route(x, oa, ob, oc, mod, lw, 512)
    pos, src, tile_expert, tile_valid = _dispatch_plan(ids.reshape(n, LANES)[:, :2], MOE_TM, ROW_TILE)
    y = _experts(tile_expert, tile_valid, src, hf.reshape(n, d), lw, MOE_TF)
    out = _combine(pos, y, x1.reshape(n, d), gates.reshape(n, LANES), mod, gfin, t)
    return out.reshape(b, t, d)


def _rope_tables(n_lat):
    rows = n_lat // GRID_W
    row = jnp.repeat(jnp.arange(rows, dtype=F32), GRID_W)
    col = jnp.tile(jnp.arange(GRID_W, dtype=F32), rows)

    def table(rot_dim):
        nf = rot_dim // 4
        inv = ROPE_THETA ** (-jnp.arange(nf, dtype=F32) / nf)
        ang_r, ang_c = row[:, None] * inv, col[:, None] * inv
        ang = jnp.concatenate([ang_r, ang_r, ang_c, ang_c], axis=-1)
        sign = jnp.tile(jnp.concatenate([-jnp.ones(nf, F32), jnp.ones(nf, F32)]), 2)
        return jnp.cos(ang), jnp.sin(ang) * sign

    cos_h, sin_h = table(HEAD_DIM)
    cos_r, sin_r = table(B_ROPE)
    one = jnp.ones((n_lat, B_NOPE), F32)
    pad = LANES - B_NOPE - B_ROPE
    cos_b = jnp.concatenate([one, cos_r, jnp.ones((n_lat, pad), F32)], axis=-1)
    sin_b = jnp.concatenate([0 * one, sin_r, jnp.zeros((n_lat, pad), F32)], axis=-1)
    return (jnp.tile(cos_h, (1, 2)), jnp.tile(sin_h, (1, 2)), cos_b, sin_b)


def _permute_heads(w, axis):
    shape = w.shape
    w = w.reshape(shape[:axis] + (6, HEAD_DIM) + shape[axis + 1:])
    w = jnp.take(w, jnp.array(HEAD_PERM), axis=axis)
    return w.reshape(shape)


def _layer_weights(i, w_in, norm_ffn, a_q_norm, a_k_norm, b_q_norm, b_kv_norm, w_uq, w_ukv, w_out):
    d = w_in.shape[1]
    wi = w_in[i]
    qk = A_HEADS * HEAD_DIM
    kv = A_KV * HEAD_DIM
    b0 = A_COLS
    c0 = A_COLS + B_COLS
    k_rope = jnp.pad(wi[:, b0 + B_Q_RANK + B_KV_RANK:c0], ((0, 0), (B_NOPE, LANES - B_NOPE - B_ROPE)))
    w_in_p = jnp.concatenate([
        _permute_heads(wi[:, :qk], 1), wi[:, qk:A_COLS],
        wi[:, b0:b0 + B_Q_RANK + B_KV_RANK], k_rope,
        _permute_heads(wi[:, c0:c0 + qk], 1), wi[:, c0 + qk:]], axis=1).astype(BF16)
    w_uq_p = jnp.pad(w_uq[i].reshape(B_Q_RANK, B_HEADS, B_NOPE + B_ROPE),
                     ((0, 0), (0, 0), (0, LANES - B_NOPE - B_ROPE))).reshape(B_Q_RANK, B_HEADS * LANES)
    ukv = w_ukv[i].reshape(B_KV_RANK, B_HEADS, B_NOPE + B_V)
    k_up = jnp.pad(ukv[:, :, :B_NOPE], ((0, 0), (0, 0), (0, LANES - B_NOPE))).reshape(B_KV_RANK, -1)
    v_up = ukv[:, :, B_NOPE:].reshape(B_KV_RANK, -1)
    wo = w_out[i]
    w_out_p = jnp.concatenate([_permute_heads(wo[:qk], 0), wo[qk:qk + B_HEADS * B_V],
                               _permute_heads(wo[qk + B_HEADS * B_V:], 0)], axis=0).astype(BF16)
    blk = (jnp.arange(LANES)[:, None] // HEAD_DIM) == (jnp.arange(LANES)[None, :] // HEAD_DIM)
    del kv, d
    return {
        "w_in": w_in_p,
        "m2": (blk.astype(F32) / HEAD_DIM).astype(BF16),
        "a_q_gain": jnp.tile(a_q_norm[i], 2)[None], "a_k_gain": jnp.tile(a_k_norm[i], 2)[None],
        "b_q_gain": b_q_norm[i][None], "b_kv_gain": b_kv_norm[i][None],
        "w_uq": w_uq_p.astype(BF16),
        "w_ukv": jnp.concatenate([k_up, v_up], axis=1).astype(BF16),
        "w_out": w_out_p,
        "g_ffn": norm_ffn[i][None],
    }


def kernel(x, c, ctx, c_ctx, w_mod, b_mod, norm_mix, norm_ffn, w_in, a_q_norm, a_k_norm, b_q_norm,
           b_kv_norm, w_uq, w_ukv, c_sinks, w_out, w1_dense, w3_dense, w2_dense, w_router, w1_moe,
           w3_moe, w2_moe, final_norm):
    bsz, n_lat, d = x.shape
    n_ctx = ctx.shape[1]
    depth = w_mod.shape[0]
    mod_rows = 16
    cc = jnp.concatenate([c, c_ctx[None], jnp.zeros((mod_rows - bsz - 1, d), F32)], axis=0)
    mods = _modulation(cc, w_mod, b_mod)
    tables = _rope_tables(n_lat)
    no_sink = jnp.zeros((A_HEADS,), F32)
    xc = ctx
    for i in range(depth):
        last = i == depth - 1
        mod_lat = mods[i, :bsz].reshape(bsz, 6, d)
        mod_ctx = mods[i, bsz:bsz + 1].reshape(1, 6, d)
        lw = _layer_weights(i, w_in, norm_ffn, a_q_norm, a_k_norm, b_q_norm, b_kv_norm, w_uq, w_ukv,
                            w_out)
        g_mix = norm_mix[i][None]
        lat = _in_projection(x, mod_lat, True, g_mix, lw, tables, True, 512)
        cx = _in_projection(xc, mod_ctx, False, g_mix, lw, tables, False, n_ctx)
        qa, ka, va, qb, kb, vb, qc, kc, vc = lat
        cqa, cka, cva, cqb, ckb, cvb, cqc, ckc, cvc = cx

        def both(u, w):
            return jnp.concatenate([u, w], axis=2)

        sinks = jnp.take(c_sinks[i], jnp.array(HEAD_PERM))
        oa = _flash(qa, both(ka, cka), both(va, cva), no_sink, (0,) * 6, (0,) * 6, False, 512)
        ob = _flash(qb, both(kb, ckb), both(vb, cvb), no_sink, (0, 1, 2, 3), (0, 0, 1, 1), False, 512)
        oc = _window_attention(qc, both(kc, ckc), both(vc, cvc), sinks, n_lat)
        if not last:
            coa = _flash(cqa, cka, cva, no_sink, (0,) * 6, (0,) * 6, False, n_ctx)
            cob = _flash(cqb, ckb, cvb, no_sink, (0, 1, 2, 3), (0, 0, 1, 1), False, n_ctx)
            coc = _flash(cqc, ckc, cvc, sinks, (0,) * 6, (0,) * 6, True, n_ctx)
        j = i // 2
        if i % 2 == 0:
            lw.update(w1=w1_dense[j].astype(BF16), w3=w3_dense[j].astype(BF16),
                      w2=w2_dense[j].astype(BF16))
            x_new = _ffn_dense(x, oa, ob, oc, mod_lat, True, lw, 512)
            if not last:
                xc = _ffn_dense(xc, coa, cob, coc, mod_ctx, False, lw, n_ctx)
            x = x_new
        else:
            assert last, "the expert channel mixer is fused with the final norm"
            lw.update(w1=w1_moe[j].astype(BF16), w3=w3_moe[j].astype(BF16), w2=w2_moe[j].astype(BF16),
                      w_router=jnp.pad(w_router[j], ((0, 0), (0, LANES - N_EXPERTS))))
            x = _ffn_moe(x, oa, ob, oc, mod_lat, lw, final_norm[None])
    return x
```

```python
import functools

import jax
import jax.numpy as jnp
from jax import lax
from jax.experimental import pallas as pl
from jax.experimental.pallas import tpu as pltpu

F32 = jnp.float32
BF16 = jnp.bfloat16

LANES = 128
HEAD_DIM = 64
GRID_W = 64
ROPE_THETA = 10000.0
EPS = 1e-6
NEG_BIG = -1e30
A_HEADS, A_KV = 6, 2
B_HEADS, B_Q_RANK, B_KV_RANK, B_NOPE, B_ROPE, B_V = 4, 256, 128, 64, 32, 64
C_HEADS, C_KV, C_WINDOW = 6, 2, 128
N_EXPERTS = 8
A_COLS = (A_HEADS + 2 * A_KV) * HEAD_DIM
B_COLS = B_Q_RANK + B_KV_RANK + B_ROPE
VMEM_LIMIT = 56 * 1024 * 1024
LOG2E = 1.4426950408889634
FLASH_TK = 2304
MXU_COLS = 256
WINDOWS_PER_STEP = 8
ISSUE_UNROLL = 8
MOE_TM = 512
MOE_TF = 1792
ROW_TILE = 1024

HEAD_PERM = tuple((j % 2) * 3 + j // 2 for j in range(6))


def _cparams(sem):
    return pltpu.CompilerParams(dimension_semantics=sem, vmem_limit_bytes=VMEM_LIMIT)


def _rms(x):
    return x * lax.rsqrt(jnp.mean(x * x, axis=-1, keepdims=True) + EPS)


def _dot(a, b):
    return jnp.dot(a, b, preferred_element_type=F32)


def _dot_nt(a, b):
    return lax.dot_general(a, b, (((1,), (1,)), ((), ())), preferred_element_type=F32)


def _split(a):
    hi = a.astype(BF16)
    return hi, (a - hi.astype(F32)).astype(BF16)


def _dot3(a, b):
    ah, al = _split(a)
    bh, bl = _split(b)
    return _dot(ah, bh) + (_dot(ah, bl) + _dot(al, bh))


def _silu(a):
    return a * jax.nn.sigmoid(a)


def _lane(shape):
    return lax.broadcasted_iota(jnp.int32, shape, len(shape) - 1)


def _mod_kernel(c_ref, w_ref, b_ref, o_ref):
    o_ref[...] = _dot3(_silu(c_ref[...]), w_ref[...]) + b_ref[...]


def _modulation(cc, w_mod, b_mod):
    depth, d, n = w_mod.shape
    rows = cc.shape[0]
    tn = 1536
    return pl.pallas_call(
        _mod_kernel,
        grid=(depth, n // tn),
        in_specs=[pl.BlockSpec((rows, d), lambda i, j: (0, 0)),
                  pl.BlockSpec((None, d, tn), lambda i, j: (i, 0, j)),
                  pl.BlockSpec((None, 1, tn), lambda i, j: (i, 0, j))],
        out_specs=pl.BlockSpec((None, rows, tn), lambda i, j: (i, 0, j)),
        out_shape=jax.ShapeDtypeStruct((depth, rows, n), F32),
        compiler_params=_cparams(("arbitrary", "arbitrary")),
        name="modulation",
    )(cc, w_mod, b_mod.reshape(depth, 1, n))


def _rope(t, cos, sin_signed, half):
    first = (_lane(t.shape) & (2 * half - 1)) < half
    rot = jnp.where(first, pltpu.roll(t, LANES - half, 1), pltpu.roll(t, half, 1))
    return t * cos + rot * sin_signed


def _store_pair(ref, p, rows, t):
    left = _lane(t.shape) < HEAD_DIM
    ref[0, 2 * p, rows] = jnp.where(left, t, 0.0).astype(BF16)
    ref[0, 2 * p + 1, rows] = jnp.where(left, 0.0, t).astype(BF16)


def _proj_kernel(*refs, use_rope, n_parts):
    part_rows = refs[0].shape[1] // n_parts
    for part in range(n_parts):
        _proj_rows(slice(part * part_rows, (part + 1) * part_rows), use_rope, *refs)


def _proj_rows(rows, use_rope, x_ref, mod_ref, g_ref, win_ref, m2_ref, gaq_ref, gak_ref, gbq_ref,
               gbkv_ref, wuq_ref, wukv_ref, cosa_ref, sina_ref, cosb_ref, sinb_ref,
               qa_ref, ka_ref, va_ref, qb_ref, kb_ref, vb_ref, qc_ref, kc_ref, vc_ref):
    mod = mod_ref[0]
    h = _rms(x_ref[0, rows]) * g_ref[...]
    h = h * (1.0 + mod[1:2]) + mod[0:1]
    proj = _dot(h.astype(BF16), win_ref[...])

    def chunk(i):
        return proj[:, i * LANES:(i + 1) * LANES]

    def head_norm(t, gain):
        ms = _dot((t * t).astype(BF16), m2_ref[...])
        return t * lax.rsqrt(ms + EPS) * gain

    def rope_a(t):
        return _rope(t, cosa_ref[rows], sina_ref[rows], HEAD_DIM // 4) if use_rope else t

    def rope_b(t):
        return _rope(t, cosb_ref[rows], sinb_ref[rows], B_ROPE // 4) if use_rope else t

    scale = HEAD_DIM ** -0.5 * LOG2E
    for p in range(3):
        _store_pair(qa_ref, p, rows, rope_a(head_norm(chunk(p), gaq_ref[...])) * scale)
    ka_ref[0, 0, rows] = rope_a(head_norm(chunk(3), gak_ref[...])).astype(BF16)
    va_ref[0, 0, rows] = chunk(4).astype(BF16)
    cq = _rms(proj[:, 5 * LANES:7 * LANES]) * gbq_ref[...]
    qb = _dot(cq.astype(BF16), wuq_ref[...])
    ckv = _rms(chunk(7)) * gbkv_ref[...]
    kvb = _dot(ckv.astype(BF16), wukv_ref[...])
    k_rope = rope_b(chunk(8))
    scale_b = (B_NOPE + B_ROPE) ** -0.5 * LOG2E
    for hd in range(B_HEADS):
        qb_ref[0, hd, rows] = (rope_b(qb[:, hd * LANES:(hd + 1) * LANES]) * scale_b).astype(BF16)
        kb_ref[0, hd, rows] = (kvb[:, hd * LANES:(hd + 1) * LANES] + k_rope).astype(BF16)
    for p in range(2):
        vb_ref[0, p, rows] = kvb[:, (B_HEADS + p) * LANES:(B_HEADS + p + 1) * LANES].astype(BF16)
    for p in range(3):
        _store_pair(qc_ref, p, rows, rope_a(chunk(9 + p)) * scale)
    kc_ref[0, 0, rows] = rope_a(chunk(12)).astype(BF16)
    vc_ref[0, 0, rows] = chunk(13).astype(BF16)


def _in_projection(x, mod, per_batch_mod, g, lw, tables, use_rope, tm):
    b, t, d = x.shape
    ncol = lw["w_in"].shape[1]
    heads = (6, 1, 1, 4, 4, 2, 6, 1, 1)

    def const(shape):
        return pl.BlockSpec(shape, lambda i, j: (0,) * len(shape))

    mod_map = (lambda i, j: (i, 0, 0)) if per_batch_mod else (lambda i, j: (0, 0, 0))
    tab = pl.BlockSpec((tm, LANES), lambda i, j: (j, 0))
    return pl.pallas_call(
        functools.partial(_proj_kernel, use_rope=use_rope, n_parts=2),
        grid=(b, t // tm),
        in_specs=[pl.BlockSpec((1, tm, d), lambda i, j: (i, j, 0)),
                  pl.BlockSpec((1, 6, d), mod_map),
                  const((1, d)), const((d, ncol)), const((LANES, LANES)),
                  const((1, LANES)), const((1, LANES)), const((1, B_Q_RANK)), const((1, B_KV_RANK)),
                  const((B_Q_RANK, 4 * LANES)), const((B_KV_RANK, 6 * LANES)),
                  tab, tab, tab, tab],
        out_specs=[pl.BlockSpec((1, n, tm, LANES), lambda i, j: (i, 0, j, 0)) for n in heads],
        out_shape=[jax.ShapeDtypeStruct((b, n, t, LANES), BF16) for n in heads],
        compiler_params=_cparams(("parallel", "parallel")),
        name="in_projection",
    )(x, mod, g, lw["w_in"], lw["m2"], lw["a_q_gain"], lw["a_k_gain"], lw["b_q_gain"],
      lw["b_kv_gain"], lw["w_uq"], lw["w_ukv"], *tables)


def _tile_chunks(t_keys, max_chunk):
    tiles = -(-t_keys // MXU_COLS)
    n = -(-tiles * MXU_COLS // max_chunk)
    sizes = [(tiles // n + (c < tiles % n)) * MXU_COLS for c in range(n)]
    sizes[-1] -= tiles * MXU_COLS - t_keys
    return tuple(sizes)


def _flash_kernel(sink_ref, q_ref, k_ref, v_ref, o_ref, *, kidx, vidx, tk, use_sink):
    nh, tq = q_ref.shape[1], q_ref.shape[2]
    t_keys = k_ref.shape[2]
    chunks = _tile_chunks(t_keys, tk)
    left = _lane((tq, LANES)) < HEAD_DIM
    for pair in range(nh // 2):
        heads = (2 * pair, 2 * pair + 1)

        def step(start, size, carry, heads=heads):
            new = []
            for j, (m, l, acc) in zip(heads, carry):
                k = k_ref[0, kidx[j], pl.ds(start, size), :]
                v = v_ref[0, vidx[j], pl.ds(start, size), :]
                s = _dot_nt(q_ref[0, j], k)
                m_new = jnp.maximum(m, jnp.max(s, axis=-1, keepdims=True))
                alpha = jnp.exp2(m - m_new)
                p = jnp.exp2(s - m_new)
                l = alpha * l + jnp.sum(p, axis=-1, keepdims=True)
                acc = alpha * acc + _dot(p.astype(BF16), v)
                new.append((m_new, l, acc))
            return tuple(new)

        def init(j):
            if use_sink:
                m, l = jnp.full((tq, 1), sink_ref[j] * LOG2E, F32), jnp.ones((tq, 1), F32)
            else:
                m, l = jnp.full((tq, 1), NEG_BIG, F32), jnp.zeros((tq, 1), F32)
            return m, l, jnp.zeros((tq, LANES), F32)

        carry = tuple(init(j) for j in heads)
        start = 0
        for size in chunks:
            carry = step(start, size, carry)
            start += size
        even, odd = (c[2] / c[1] for c in carry)
        o_ref[0, pair] = jnp.where(left, even, odd).astype(BF16)


def _flash(q, k, v, sinks, kidx, vidx, use_sink, tq):
    b, nh, t, _ = q.shape
    nk, tkeys = k.shape[1], k.shape[2]
    nv = v.shape[1]
    return pl.pallas_call(
        functools.partial(_flash_kernel, kidx=kidx, vidx=vidx, tk=FLASH_TK, use_sink=use_sink),
        grid=(b, t // tq),
        in_specs=[pl.BlockSpec(memory_space=pltpu.SMEM),
                  pl.BlockSpec((1, nh, tq, LANES), lambda i, j: (i, 0, j, 0)),
                  pl.BlockSpec((1, nk, tkeys, LANES), lambda i, j: (i, 0, 0, 0)),
                  pl.BlockSpec((1, nv, tkeys, LANES), lambda i, j: (i, 0, 0, 0))],
        out_specs=pl.BlockSpec((1, nh // 2, tq, LANES), lambda i, j: (i, 0, j, 0)),
        out_shape=jax.ShapeDtypeStruct((b, nh // 2, t, LANES), BF16),
        compiler_params=_cparams(("parallel", "parallel")),
        name="flash_attention",
    )(sinks, q, k, v)


def _window_block(sink, q, ks, vs, has_prev, has_next, w):
    k = jnp.concatenate(ks, axis=0)
    v = jnp.concatenate(vs, axis=0)
    s = _dot_nt(q, k)
    qq = lax.broadcasted_iota(jnp.int32, s.shape, 0) & (w - 1)
    col = lax.broadcasted_iota(jnp.int32, s.shape, 1)
    never = 4 * w
    off_prev = jnp.where(has_prev, 0, never)
    off_next = jnp.where(has_next, 0, never)
    valid = ((col < w) & (col >= qq + off_prev)) | ((col >= w) & (col < 2 * w)) \
        | ((col >= 2 * w) & (col < 3 * w) & (col - 2 * w + off_next <= qq)) | (col >= 3 * w)
    s = jnp.where(valid, s, NEG_BIG)
    m = jnp.maximum(sink, jnp.max(s, axis=-1, keepdims=True))
    e = jnp.exp2(s - m)
    denom = jnp.exp2(sink - m) + jnp.sum(e, axis=-1, keepdims=True)
    return _dot(e.astype(BF16), v) / denom


def _window_kernel(sink_ref, q_ref, kp_ref, kc_ref, kn_ref, kx_ref, vp_ref, vc_ref, vn_ref, vx_ref,
                   o_ref):
    nh, w = q_ref.shape[1], kp_ref.shape[2]
    n_win = q_ref.shape[2] // w
    i, steps = pl.program_id(1), pl.num_programs(1)
    sink = jnp.concatenate([jnp.full((w, 1), sink_ref[j] * LOG2E, F32) for j in range(nh)], axis=0)
    ks = [kp_ref[0, 0]] + [kc_ref[0, 0, c * w:(c + 1) * w] for c in range(n_win)] + [kn_ref[0, 0]]
    vs = [vp_ref[0, 0]] + [vc_ref[0, 0, c * w:(c + 1) * w] for c in range(n_win)] + [vn_ref[0, 0]]
    kx, vx = kx_ref[0, 0], vx_ref[0, 0]
    left = _lane((w, LANES)) < HEAD_DIM
    for c in range(n_win):
        rows = slice(c * w, (c + 1) * w)
        has_prev = (i >= 1) if c == 0 else True
        has_next = (i <= steps - 2) if c == n_win - 1 else True
        o = _window_block(sink, q_ref[0, :, rows].reshape(nh * w, LANES), ks[c:c + 3] + [kx],
                          vs[c:c + 3] + [vx], has_prev, has_next, w)
        for p in range(nh // 2):
            o_ref[0, p, rows] = jnp.where(left, o[2 * p * w:(2 * p + 1) * w],
                                          o[(2 * p + 1) * w:(2 * p + 2) * w]).astype(BF16)


def _window_attention(q, k_all, v_all, sinks, n_lat):
    b, nh, _, _ = q.shape
    w = C_WINDOW
    nb = n_lat // w
    n_ctx = k_all.shape[2] - n_lat
    g = max(c for c in range(1, WINDOWS_PER_STEP + 1) if nb % c == 0)
    assert n_lat % n_ctx == 0 and w & (w - 1) == 0
    prev = pl.BlockSpec((1, 1, w, LANES), lambda i, j: (i, 0, jnp.maximum(g * j - 1, 0), 0))
    cur = pl.BlockSpec((1, 1, g * w, LANES), lambda i, j: (i, 0, j, 0))
    nxt = pl.BlockSpec((1, 1, w, LANES), lambda i, j: (i, 0, jnp.minimum(g * j + g, nb - 1), 0))
    ctx = pl.BlockSpec((1, 1, n_ctx, LANES), lambda i, j: (i, 0, n_lat // n_ctx, 0))
    return pl.pallas_call(
        _window_kernel,
        grid=(b, nb // g),
        in_specs=[pl.BlockSpec(memory_space=pltpu.SMEM),
                  pl.BlockSpec((1, nh, g * w, LANES), lambda i, j: (i, 0, j, 0)),
                  prev, cur, nxt, ctx, prev, cur, nxt, ctx],
        out_specs=pl.BlockSpec((1, nh // 2, g * w, LANES), lambda i, j: (i, 0, j, 0)),
        out_shape=jax.ShapeDtypeStruct((b, nh // 2, n_lat, LANES), BF16),
        compiler_params=_cparams(("parallel", "parallel")),
        name="window_attention",
    )(sinks, q, k_all, k_all, k_all, k_all, v_all, v_all, v_all, v_all)


def _mixed_residual(x_ref, oa_ref, ob_ref, oc_ref, mod, wout_ref, rows=slice(None)):
    att = jnp.concatenate([oa_ref[0, 0, rows], oa_ref[0, 1, rows], oa_ref[0, 2, rows],
                           ob_ref[0, 0, rows], ob_ref[0, 1, rows],
                           oc_ref[0, 0, rows], oc_ref[0, 1, rows], oc_ref[0, 2, rows]], axis=-1)
    return x_ref[0, rows] + mod[2:3] * _dot(att, wout_ref[...])


def _ffn_dense_kernel(x_ref, oa_ref, ob_ref, oc_ref, mod_ref, wout_ref, g_ref, w1_ref, w3_ref, w2_ref,
                      o_ref, *, n_chunks):
    mod = mod_ref[0]
    x1 = _mixed_residual(x_ref, oa_ref, ob_ref, oc_ref, mod, wout_ref)
    hf = ((_rms(x1) * g_ref[...]) * (1.0 + mod[4:5]) + mod[3:4]).astype(BF16)
    y = jnp.zeros(x1.shape, F32)
    start = 0
    for size in _tile_chunks(w1_ref.shape[1], -(-w1_ref.shape[1] // n_chunks)):
        cols = slice(start, start + size)
        a = _dot(hf, w1_ref[:, cols])
        gate = _dot(hf, w3_ref[:, cols])
        y = y + _dot((_silu(a) * gate).astype(BF16), w2_ref[cols, :])
        start += size
    o_ref[0] = x1 + mod[5:6] * y


def _resident(shape):
    return pl.BlockSpec(shape, lambda i, j: (0,) * len(shape), pipeline_mode=pl.Buffered(1))


def _ffn_dense(x, oa, ob, oc, mod, per_batch_mod, lw, tm):
    b, t, d = x.shape
    f = lw["w1"].shape[1]
    mod_map = (lambda i, j: (i, 0, 0)) if per_batch_mod else (lambda i, j: (0, 0, 0))

    def att(n):
        return pl.BlockSpec((1, n, tm, LANES), lambda i, j: (i, 0, j, 0))

    return pl.pallas_call(
        functools.partial(_ffn_dense_kernel, n_chunks=2),
        grid=(b, t // tm),
        in_specs=[pl.BlockSpec((1, tm, d), lambda i, j: (i, j, 0)), att(3), att(2), att(3),
                  pl.BlockSpec((1, 6, d), mod_map),
                  _resident((d, d)), _resident((1, d)),
                  _resident((d, f)), _resident((d, f)), _resident((f, d))],
        out_specs=pl.BlockSpec((1, tm, d), lambda i, j: (i, j, 0)),
        out_shape=jax.ShapeDtypeStruct((b, t, d), F32),
        compiler_params=_cparams(("parallel", "parallel")),
        name="ffn_dense",
    )(x, oa, ob, oc, mod, lw["w_out"], lw["g_ffn"], lw["w1"], lw["w3"], lw["w2"])


def _route_kernel(x_ref, oa_ref, ob_ref, oc_ref, mod_ref, wout_ref, g_ref, wr_ref,
                  x1_ref, hf_ref, ids_ref, gates_ref):
    half = x_ref.shape[1] // 2
    for part in range(2):
        _route_rows(slice(part * half, (part + 1) * half), x_ref, oa_ref, ob_ref, oc_ref, mod_ref,
                    wout_ref, g_ref, wr_ref, x1_ref, hf_ref, ids_ref, gates_ref)


def _route_rows(rows, x_ref, oa_ref, ob_ref, oc_ref, mod_ref, wout_ref, g_ref, wr_ref,
                x1_ref, hf_ref, ids_ref, gates_ref):
    mod = mod_ref[0]
    x1 = _mixed_residual(x_ref, oa_ref, ob_ref, oc_ref, mod, wout_ref, rows)
    hf = (_rms(x1) * g_ref[...]) * (1.0 + mod[4:5]) + mod[3:4]
    x1_ref[0, rows] = x1
    hf_ref[0, rows] = hf
    logits = _dot3(hf, wr_ref[...])
    lane = _lane(logits.shape)
    logits = jnp.where(lane < N_EXPERTS, logits, NEG_BIG)
    m1 = jnp.max(logits, axis=-1, keepdims=True)
    i1 = jnp.min(jnp.where(logits == m1, lane, LANES), axis=-1, keepdims=True)
    rest = jnp.where(lane == i1, NEG_BIG, logits)
    m2 = jnp.max(rest, axis=-1, keepdims=True)
    i2 = jnp.min(jnp.where(rest == m2, lane, LANES), axis=-1, keepdims=True)
    e2 = jnp.exp(m2 - m1)
    g1 = 1.0 / (1.0 + e2)
    ids_ref[0, rows] = jnp.where(lane == 0, i1, jnp.where(lane == 1, i2, 0))
    gates_ref[0, rows] = jnp.where(lane == 0, g1, jnp.where(lane == 1, e2 * g1, 0.0))


def _route(x, oa, ob, oc, mod, lw, tm):
    b, t, d = x.shape

    def att(n):
        return pl.BlockSpec((1, n, tm, LANES), lambda i, j: (i, 0, j, 0))

    def const(shape):
        return pl.BlockSpec(shape, lambda i, j: (0,) * len(shape))

    tok = pl.BlockSpec((1, tm, d), lambda i, j: (i, j, 0))
    meta = pl.BlockSpec((1, tm, LANES), lambda i, j: (i, j, 0))
    return pl.pallas_call(
        _route_kernel,
        grid=(b, t // tm),
        in_specs=[tok, att(3), att(2), att(3), pl.BlockSpec((1, 6, d), lambda i, j: (i, 0, 0)),
                  const((d, d)), const((1, d)), const((d, LANES))],
        out_specs=[tok, tok, meta, meta],
        out_shape=[jax.ShapeDtypeStruct((b, t, d), F32), jax.ShapeDtypeStruct((b, t, d), F32),
                   jax.ShapeDtypeStruct((b, t, LANES), jnp.int32),
                   jax.ShapeDtypeStruct((b, t, LANES), F32)],
        compiler_params=_cparams(("parallel", "parallel")),
        name="moe_route",
    )(x, oa, ob, oc, mod, lw["w_out"], lw["g_ffn"], lw["w_router"])


def _row_copy(src_ref, src_row, dst_ref, dst_row, sem):
    return pltpu.make_async_copy(src_ref.at[pl.ds(src_row, 1)], dst_ref.at[pl.ds(dst_row, 1)], sem)


def _load_indices(idx_ref, tile, idx_smem, sem):
    cp = pltpu.make_async_copy(idx_ref.at[tile], idx_smem, sem)
    cp.start()
    cp.wait()


def _wait_rows(src_ref, dst_ref, sem):
    pltpu.make_async_copy(src_ref.at[pl.ds(0, dst_ref.shape[0])], dst_ref, sem).wait()


def _experts_kernel(te_ref, tv_ref, src_ref, hf_ref, w1_ref, w3_ref, w2_ref, y_ref,
                    src_smem, xg_scr, xb_scr, acc_scr, src_sem, sems):
    i, f = pl.program_id(0), pl.program_id(1)
    tm = xb_scr.shape[0]

    def index_copy(tile):
        return pltpu.make_async_copy(src_ref.at[tile], src_smem, src_sem)

    def issue_rows(tile):
        slot = tile % 2

        def issue(r, carry):
            _row_copy(hf_ref, src_smem[r], xg_scr.at[slot], r, sems.at[slot]).start()
            return carry

        lax.fori_loop(0, tm, issue, 0, unroll=ISSUE_UNROLL)

    nxt = jnp.minimum(i + 1, pl.num_programs(0) - 1)
    first_step = f == 0
    prefetch = first_step & (i + 1 < pl.num_programs(0)) & (tv_ref[nxt] > 0)

    @pl.when(first_step & (i == 0))
    def _():
        index_copy(0).start()
        index_copy(0).wait()
        issue_rows(0)

    @pl.when(prefetch)
    def _():
        index_copy(nxt).start()

    @pl.when(first_step & (tv_ref[i] > 0))
    def _():
        slot = i % 2
        _wait_rows(hf_ref, xg_scr.at[slot], sems.at[slot])
        xb_scr[...] = xg_scr[slot].astype(BF16)

    @pl.when(first_step)
    def _():
        acc_scr[...] = jnp.zeros(acc_scr.shape, F32)

    @pl.when(prefetch)
    def _():
        index_copy(nxt).wait()
        issue_rows(nxt)

    @pl.when(tv_ref[i] > 0)
    def _():
        xb = xb_scr[...]
        a = _dot(xb, w1_ref[...])
        gate = _dot(xb, w3_ref[...])
        acc_scr[...] += _dot((_silu(a) * gate).astype(BF16), w2_ref[...])

    @pl.when(f == pl.num_programs(1) - 1)
    def _():
        y_ref[...] = acc_scr[...]


def _experts(tile_expert, tile_valid, src, hf, lw, tf):
    n_tiles, tm = src.shape
    d = hf.shape[1]
    f = lw["w1"].shape[2]
    grid_spec = pltpu.PrefetchScalarGridSpec(
        num_scalar_prefetch=2,
        grid=(n_tiles, f // tf),
        in_specs=[pl.BlockSpec(memory_space=pltpu.VMEM), pl.BlockSpec(memory_space=pl.ANY),
                  pl.BlockSpec((None, d, tf), lambda i, k, te, tv: (te[i], 0, k)),
                  pl.BlockSpec((None, d, tf), lambda i, k, te, tv: (te[i], 0, k)),
                  pl.BlockSpec((None, tf, d), lambda i, k, te, tv: (te[i], k, 0))],
        out_specs=pl.BlockSpec((tm, d), lambda i, k, te, tv: (i, 0)),
        scratch_shapes=[pltpu.SMEM((tm,), jnp.int32), pltpu.VMEM((2, tm, d), F32),
                        pltpu.VMEM((tm, d), BF16), pltpu.VMEM((tm, d), F32),
                        pltpu.SemaphoreType.DMA(()), pltpu.SemaphoreType.DMA((2,))])
    return pl.pallas_call(
        _experts_kernel,
        grid_spec=grid_spec,
        out_shape=jax.ShapeDtypeStruct((n_tiles * tm, d), F32),
        compiler_params=_cparams(("arbitrary", "arbitrary")),
        name="moe_experts",
    )(tile_expert, tile_valid, src, hf, lw["w1"], lw["w3"], lw["w2"])


def _combine_kernel(pos_ref, y_ref, x1_ref, gates_ref, mod_ref, gfin_ref, o_ref,
                    pos_smem, y0_scr, y1_scr, pos_sem, sem):
    rows = y0_scr.shape[0]
    _load_indices(pos_ref, pl.program_id(0), pos_smem, pos_sem)

    def issue(r, carry):
        _row_copy(y_ref, pos_smem[r], y0_scr, r, sem).start()
        _row_copy(y_ref, pos_smem[rows + r], y1_scr, r, sem).start()
        return carry

    lax.fori_loop(0, rows, issue, 0, unroll=ISSUE_UNROLL)
    _wait_rows(y_ref, y0_scr, sem)
    _wait_rows(y_ref, y1_scr, sem)
    gates = gates_ref[...]
    y = gates[:, 0:1] * y0_scr[...] + gates[:, 1:2] * y1_scr[...]
    o_ref[...] = _rms(x1_ref[...] + mod_ref[0, 5:6] * y) * gfin_ref[...]


def _combine(pos, y, x1, gates, mod, gfin, tokens_per_batch):
    n, d = x1.shape
    n_tiles, two_rows = pos.shape
    rows = two_rows // 2
    tok = pl.BlockSpec((rows, d), lambda i: (i, 0))
    return pl.pallas_call(
        _combine_kernel,
        grid=(n_tiles,),
        in_specs=[pl.BlockSpec(memory_space=pltpu.VMEM), pl.BlockSpec(memory_space=pl.ANY), tok,
                  pl.BlockSpec((rows, LANES), lambda i: (i, 0)),
                  pl.BlockSpec((1, 6, d), lambda i: (i * rows // tokens_per_batch, 0, 0)),
                  pl.BlockSpec((1, d), lambda i: (0, 0))],
        out_specs=tok,
        out_shape=jax.ShapeDtypeStruct((n, d), F32),
        scratch_shapes=[pltpu.SMEM((two_rows,), jnp.int32), pltpu.VMEM((rows, d), F32),
                        pltpu.VMEM((rows, d), F32), pltpu.SemaphoreType.DMA(()),
                        pltpu.SemaphoreType.DMA(())],
        compiler_params=_cparams(("arbitrary",)),
        name="moe_combine",
    )(pos, y, x1, gates, mod, gfin)


def _dispatch_plan(ids, tm, rows):
    n = ids.shape[0]
    flat = ids.reshape(-1)
    onehot = (flat[None, :] == jnp.arange(N_EXPERTS)[:, None]).astype(jnp.int32)
    csum = jnp.cumsum(onehot, axis=1)
    rank = jnp.sum(csum * onehot, axis=0) - 1
    counts = csum[:, -1]
    padded = (counts + tm - 1) // tm * tm
    ends = jnp.cumsum(padded)
    pos = (ends - padded)[flat] + rank
    n_sorted = 2 * n + N_EXPERTS * tm
    starts = jnp.arange(n_sorted // tm, dtype=jnp.int32) * tm
    tile_expert = jnp.sum((starts[:, None] >= ends[None, :]).astype(jnp.int32), axis=1)
    tile_valid = (starts < ends[-1]).astype(jnp.int32)
    last_expert = tile_expert[jnp.maximum(ends[-1] // tm - 1, 0)]
    tile_expert = jnp.where(tile_valid > 0, tile_expert, last_expert)
    order = jnp.argsort(flat, stable=True).astype(jnp.int32)
    first = jnp.cumsum(counts) - counts
    row = jnp.arange(n_sorted, dtype=jnp.int32)
    row_expert = jnp.repeat(tile_expert, tm)
    r = row - (ends - padded)[row_expert]
    live = (r < counts[row_expert]) & (jnp.repeat(tile_valid, tm) > 0)
    src = jnp.where(live, order[jnp.clip(first[row_expert] + r, 0, 2 * n - 1)] // 2, 0)
    pos = pos.reshape(n // rows, rows, 2).transpose(0, 2, 1).reshape(n // rows, 2 * rows)
    return pos.astype(jnp.int32), src.reshape(n_sorted // tm, tm), tile_expert, tile_valid


def _ffn_moe(x, oa, ob, oc, mod, lw, gfin):
    b, t, d = x.shape
    n = b * t
    assert t % ROW_TILE == 0
    x1, hf, ids, gates = _route(x, oa, ob, oc, mod, lw, 512)
    pos, src, tile_expert, tile_valid = _dispatch_plan(ids.reshape(n, LANES)[:, :2], MOE_TM, ROW_TILE)
    y = _experts(tile_expert, tile_valid, src, hf.reshape(n, d), lw, MOE_TF)
    out = _combine(pos, y, x1.reshape(n, d), gates.reshape(n, LANES), mod, gfin, t)
    return out.reshape(b, t, d)


def _rope_tables(n_lat):
    rows = n_lat // GRID_W
    row = jnp.repeat(jnp.arange(rows, dtype=F32), GRID_W)
    col = jnp.tile(jnp.arange(GRID_W, dtype=F32), rows)

    def table(rot_dim):
        nf = rot_dim // 4
        inv = ROPE_THETA ** (-jnp.arange(nf, dtype=F32) / nf)
        ang_r, ang_c = row[:, None] * inv, col[:, None] * inv
        ang = jnp.concatenate([ang_r, ang_r, ang_c, ang_c], axis=-1)
        sign = jnp.tile(jnp.concatenate([-jnp.ones(nf, F32), jnp.ones(nf, F32)]), 2)
        return jnp.cos(ang), jnp.sin(ang) * sign

    cos_h, sin_h = table(HEAD_DIM)
    cos_r, sin_r = table(B_ROPE)
    one = jnp.ones((n_lat, B_NOPE), F32)
    pad = LANES - B_NOPE - B_ROPE
    cos_b = jnp.concatenate([one, cos_r, jnp.ones((n_lat, pad), F32)], axis=-1)
    sin_b = jnp.concatenate([0 * one, sin_r, jnp.zeros((n_lat, pad), F32)], axis=-1)
    return (jnp.tile(cos_h, (1, 2)), jnp.tile(sin_h, (1, 2)), cos_b, sin_b)


def _permute_heads(w, axis):
    shape = w.shape
    w = w.reshape(shape[:axis] + (6, HEAD_DIM) + shape[axis + 1:])
    w = jnp.take(w, jnp.array(HEAD_PERM), axis=axis)
    return w.reshape(shape)


def _layer_weights(i, w_in, norm_ffn, a_q_norm, a_k_norm, b_q_norm, b_kv_norm, w_uq, w_ukv, w_out):
    d = w_in.shape[1]
    wi = w_in[i]
    qk = A_HEADS * HEAD_DIM
    kv = A_KV * HEAD_DIM
    b0 = A_COLS
    c0 = A_COLS + B_COLS
    k_rope = jnp.pad(wi[:, b0 + B_Q_RANK + B_KV_RANK:c0], ((0, 0), (B_NOPE, LANES - B_NOPE - B_ROPE)))
    w_in_p = jnp.concatenate([
        _permute_heads(wi[:, :qk], 1), wi[:, qk:A_COLS],
        wi[:, b0:b0 + B_Q_RANK + B_KV_RANK], k_rope,
        _permute_heads(wi[:, c0:c0 + qk], 1), wi[:, c0 + qk:]], axis=1).astype(BF16)
    w_uq_p = jnp.pad(w_uq[i].reshape(B_Q_RANK, B_HEADS, B_NOPE + B_ROPE),
                     ((0, 0), (0, 0), (0, LANES - B_NOPE - B_ROPE))).reshape(B_Q_RANK, B_HEADS * LANES)
    ukv = w_ukv[i].reshape(B_KV_RANK, B_HEADS, B_NOPE + B_V)
    k_up = jnp.pad(ukv[:, :, :B_NOPE], ((0, 0), (0, 0), (0, LANES - B_NOPE))).reshape(B_KV_RANK, -1)
    v_up = ukv[:, :, B_NOPE:].reshape(B_KV_RANK, -1)
    wo = w_out[i]
    w_out_p = jnp.concatenate([_permute_heads(wo[:qk], 0), wo[qk:qk + B_HEADS * B_V],
                               _permute_heads(wo[qk + B_HEADS * B_V:], 0)], axis=0).astype(BF16)
    blk = (jnp.arange(LANES)[:, None] // HEAD_DIM) == (jnp.arange(LANES)[None, :] // HEAD_DIM)
    del kv, d
    return {
        "w_in": w_in_p,
        "m2": (blk.astype(F32) / HEAD_DIM).astype(BF16),
        "a_q_gain": jnp.tile(a_q_norm[i], 2)[None], "a_k_gain": jnp.tile(a_k_norm[i], 2)[None],
        "b_q_gain": b_q_norm[i][None], "b_kv_gain": b_kv_norm[i][None],
        "w_uq": w_uq_p.astype(BF16),
        "w_ukv": jnp.concatenate([k_up, v_up], axis=1).astype(BF16),
        "w_out": w_out_p,
        "g_ffn": norm_ffn[i][None],
    }


def kernel(x, c, ctx, c_ctx, w_mod, b_mod, norm_mix, norm_ffn, w_in, a_q_norm, a_k_norm, b_q_norm,
           b_kv_norm, w_uq, w_ukv, c_sinks, w_out, w1_dense, w3_dense, w2_dense, w_router, w1_moe,
           w3_moe, w2_moe, final_norm):
    bsz, n_lat, d = x.shape
    n_ctx = ctx.shape[1]
    depth = w_mod.shape[0]
    mod_rows = 16
    cc = jnp.concatenate([c, c_ctx[None], jnp.zeros((mod_rows - bsz - 1, d), F32)], axis=0)
    mods = _modulation(cc, w_mod, b_mod)
    tables = _rope_tables(n_lat)
    no_sink = jnp.zeros((A_HEADS,), F32)
    xc = ctx
    for i in range(depth):
        last = i == depth - 1
        mod_lat = mods[i, :bsz].reshape(bsz, 6, d)
        mod_ctx = mods[i, bsz:bsz + 1].reshape(1, 6, d)
        lw = _layer_weights(i, w_in, norm_ffn, a_q_norm, a_k_norm, b_q_norm, b_kv_norm, w_uq, w_ukv,
                            w_out)
        g_mix = norm_mix[i][None]
        lat = _in_projection(x, mod_lat, True, g_mix, lw, tables, True, 512)
        cx = _in_projection(xc, mod_ctx, False, g_mix, lw, tables, False, n_ctx)
        qa, ka, va, qb, kb, vb, qc, kc, vc = lat
        cqa, cka, cva, cqb, ckb, cvb, cqc, ckc, cvc = cx

        def both(u, w):
            return jnp.concatenate([u, w], axis=2)

        sinks = jnp.take(c_sinks[i], jnp.array(HEAD_PERM))
        oa = _flash(qa, both(ka, cka), both(va, cva), no_sink, (0,) * 6, (0,) * 6, False, 512)
        ob = _flash(qb, both(kb, ckb), both(vb, cvb), no_sink, (0, 1, 2, 3), (0, 0, 1, 1), False, 512)
        oc = _window_attention(qc, both(kc, ckc), both(vc, cvc), sinks, n_lat)
        if not last:
            coa = _flash(cqa, cka, cva, no_sink, (0,) * 6, (0,) * 6, False, n_ctx)
            cob = _flash(cqb, ckb, cvb, no_sink, (0, 1, 2, 3), (0, 0, 1, 1), False, n_ctx)
            coc = _flash(cqc, ckc, cvc, sinks, (0,) * 6, (0,) * 6, True, n_ctx)
        j = i // 2
        if i % 2 == 0:
            lw.update(w1=w1_dense[j].astype(BF16), w3=w3_dense[j].astype(BF16),
                      w2=w2_dense[j].astype(BF16))
            x_new = _ffn_dense(x, oa, ob, oc, mod_lat, True, lw, 512)
            if not last:
                xc = _ffn_dense(xc, coa, cob, coc, mod_ctx, False, lw, n_ctx)
            x = x_new
        else:
            assert last, "the expert channel mixer is fused with the final norm"
            lw.update(w1=w1_moe[j].astype(BF16), w3=w3_moe[j].astype(BF16), w2=w2_moe[j].astype(BF16),
                      w_router=jnp.pad(w_router[j], ((0, 0), (0, LANES - N_EXPERTS))))
            x = _ffn_moe(x, oa, ob, oc, mod_lat, lw, final_norm[None])
    return x
```

```python
import functools

import jax
import jax.numpy as jnp
from jax import lax
from jax.experimental import pallas as pl
from jax.experimental.pallas import tpu as pltpu

F32 = jnp.float32
BF16 = jnp.bfloat16

LANES = 128
HEAD_DIM = 64
GRID_W = 64
ROPE_THETA = 10000.0
EPS = 1e-6
NEG_BIG = -1e30
A_HEADS, A_KV = 6, 2
B_HEADS, B_Q_RANK, B_KV_RANK, B_NOPE, B_ROPE, B_V = 4, 256, 128, 64, 32, 64
C_HEADS, C_KV, C_WINDOW = 6, 2, 128
N_EXPERTS = 8
A_COLS = (A_HEADS + 2 * A_KV) * HEAD_DIM
B_COLS = B_Q_RANK + B_KV_RANK + B_ROPE
VMEM_LIMIT = 56 * 1024 * 1024
LOG2E = 1.4426950408889634
FLASH_TK = 2304
MXU_COLS = 256
WINDOWS_PER_STEP = 8
ISSUE_UNROLL = 8
MOE_TM = 512
MOE_TF = 1792
ROW_TILE = 1024

HEAD_PERM = tuple((j % 2) * 3 + j // 2 for j in range(6))


def _cparams(sem):
    return pltpu.CompilerParams(dimension_semantics=sem, vmem_limit_bytes=VMEM_LIMIT)


def _rms(x):
    return x * lax.rsqrt(jnp.mean(x * x, axis=-1, keepdims=True) + EPS)


def _dot(a, b):
    return jnp.dot(a, b, preferred_element_type=F32)


def _dot_nt(a, b):
    return lax.dot_general(a, b, (((1,), (1,)), ((), ())), preferred_element_type=F32)


def _split(a):
    hi = a.astype(BF16)
    return hi, (a - hi.astype(F32)).astype(BF16)


def _dot3(a, b):
    ah, al = _split(a)
    bh, bl = _split(b)
    return _dot(ah, bh) + (_dot(ah, bl) + _dot(al, bh))


def _silu(a):
    return a * jax.nn.sigmoid(a)


def _lane(shape):
    return lax.broadcasted_iota(jnp.int32, shape, len(shape) - 1)


def _mod_kernel(c_ref, w_ref, b_ref, o_ref):
    o_ref[...] = _dot3(_silu(c_ref[...]), w_ref[...]) + b_ref[...]


def _modulation(cc, w_mod, b_mod):
    depth, d, n = w_mod.shape
    rows = cc.shape[0]
    tn = 1536
    return pl.pallas_call(
        _mod_kernel,
        grid=(depth, n // tn),
        in_specs=[pl.BlockSpec((rows, d), lambda i, j: (0, 0)),
                  pl.BlockSpec((None, d, tn), lambda i, j: (i, 0, j)),
                  pl.BlockSpec((None, 1, tn), lambda i, j: (i, 0, j))],
        out_specs=pl.BlockSpec((None, rows, tn), lambda i, j: (i, 0, j)),
        out_shape=jax.ShapeDtypeStruct((depth, rows, n), F32),
        compiler_params=_cparams(("arbitrary", "arbitrary")),
        name="modulation",
    )(cc, w_mod, b_mod.reshape(depth, 1, n))


def _rope(t, cos, sin_signed, half):
    first = (_lane(t.shape) & (2 * half - 1)) < half
    rot = jnp.where(first, pltpu.roll(t, LANES - half, 1), pltpu.roll(t, half, 1))
    return t * cos + rot * sin_signed


def _store_pair(ref, p, rows, t):
    left = _lane(t.shape) < HEAD_DIM
    ref[0, 2 * p, rows] = jnp.where(left, t, 0.0).astype(BF16)
    ref[0, 2 * p + 1, rows] = jnp.where(left, 0.0, t).astype(BF16)


def _proj_kernel(*refs, use_rope, n_parts):
    part_rows = refs[0].shape[1] // n_parts
    for part in range(n_parts):
        _proj_rows(slice(part * part_rows, (part + 1) * part_rows), use_rope, *refs)


def _proj_rows(rows, use_rope, x_ref, mod_ref, g_ref, win_ref, m2_ref, gaq_ref, gak_ref, gbq_ref,
               gbkv_ref, wuq_ref, wukv_ref, cosa_ref, sina_ref, cosb_ref, sinb_ref,
               qa_ref, ka_ref, va_ref, qb_ref, kb_ref, vb_ref, qc_ref, kc_ref, vc_ref):
    mod = mod_ref[0]
    h = _rms(x_ref[0, rows]) * g_ref[...]
    h = h * (1.0 + mod[1:2]) + mod[0:1]
    proj = _dot(h.astype(BF16), win_ref[...])

    def chunk(i):
        return proj[:, i * LANES:(i + 1) * LANES]

    def head_norm(t, gain):
        ms = _dot((t * t).astype(BF16), m2_ref[...])
        return t * lax.rsqrt(ms + EPS) * gain

    def rope_a(t):
        return _rope(t, cosa_ref[rows], sina_ref[rows], HEAD_DIM // 4) if use_rope else t

    def rope_b(t):
        return _rope(t, cosb_ref[rows], sinb_ref[rows], B_ROPE // 4) if use_rope else t

    scale = HEAD_DIM ** -0.5 * LOG2E
    for p in range(3):
        _store_pair(qa_ref, p, rows, rope_a(head_norm(chunk(p), gaq_ref[...])) * scale)
    ka_ref[0, 0, rows] = rope_a(head_norm(chunk(3), gak_ref[...])).astype(BF16)
    va_ref[0, 0, rows] = chunk(4).astype(BF16)
    cq = _rms(proj[:, 5 * LANES:7 * LANES]) * gbq_ref[...]
    qb = _dot(cq.astype(BF16), wuq_ref[...])
    ckv = _rms(chunk(7)) * gbkv_ref[...]
    kvb = _dot(ckv.astype(BF16), wukv_ref[...])
    k_rope = rope_b(chunk(8))
    scale_b = (B_NOPE + B_ROPE) ** -0.5 * LOG2E
    for hd in range(B_HEADS):
        qb_ref[0, hd, rows] = (rope_b(qb[:, hd * LANES:(hd + 1) * LANES]) * scale_b).astype(BF16)
        kb_ref[0, hd, rows] = (kvb[:, hd * LANES:(hd + 1) * LANES] + k_rope).astype(BF16)
    for p in range(2):
        vb_ref[0, p, rows] = kvb[:, (B_HEADS + p) * LANES:(B_HEADS + p + 1) * LANES].astype(BF16)
    for p in range(3):
        _store_pair(qc_ref, p, rows, rope_a(chunk(9 + p)) * scale)
    kc_ref[0, 0, rows] = rope_a(chunk(12)).astype(BF16)
    vc_ref[0, 0, rows] = chunk(13).astype(BF16)


def _in_projection(x, mod, per_batch_mod, g, lw, tables, use_rope, tm):
    b, t, d = x.shape
    ncol = lw["w_in"].shape[1]
    heads = (6, 1, 1, 4, 4, 2, 6, 1, 1)

    def const(shape):
        return pl.BlockSpec(shape, lambda i, j: (0,) * len(shape))

    mod_map = (lambda i, j: (i, 0, 0)) if per_batch_mod else (lambda i, j: (0, 0, 0))
    tab = pl.BlockSpec((tm, LANES), lambda i, j: (j, 0))
    return pl.pallas_call(
        functools.partial(_proj_kernel, use_rope=use_rope, n_parts=2),
        grid=(b, t // tm),
        in_specs=[pl.BlockSpec((1, tm, d), lambda i, j: (i, j, 0)),
                  pl.BlockSpec((1, 6, d), mod_map),
                  const((1, d)), const((d, ncol)), const((LANES, LANES)),
                  const((1, LANES)), const((1, LANES)), const((1, B_Q_RANK)), const((1, B_KV_RANK)),
                  const((B_Q_RANK, 4 * LANES)), const((B_KV_RANK, 6 * LANES)),
                  tab, tab, tab, tab],
        out_specs=[pl.BlockSpec((1, n, tm, LANES), lambda i, j: (i, 0, j, 0)) for n in heads],
        out_shape=[jax.ShapeDtypeStruct((b, n, t, LANES), BF16) for n in heads],
        compiler_params=_cparams(("parallel", "parallel")),
        name="in_projection",
    )(x, mod, g, lw["w_in"], lw["m2"], lw["a_q_gain"], lw["a_k_gain"], lw["b_q_gain"],
      lw["b_kv_gain"], lw["w_uq"], lw["w_ukv"], *tables)


def _tile_chunks(t_keys, max_chunk):
    tiles = -(-t_keys // MXU_COLS)
    n = -(-tiles * MXU_COLS // max_chunk)
    sizes = [(tiles // n + (c < tiles % n)) * MXU_COLS for c in range(n)]
    sizes[-1] -= tiles * MXU_COLS - t_keys
    return tuple(sizes)


def _flash_kernel(sink_ref, q_ref, k_ref, v_ref, o_ref, *, kidx, vidx, tk, use_sink):
    nh, tq = q_ref.shape[1], q_ref.shape[2]
    t_keys = k_ref.shape[2]
    chunks = _tile_chunks(t_keys, tk)
    left = _lane((tq, LANES)) < HEAD_DIM
    for pair in range(nh // 2):
        heads = (2 * pair, 2 * pair + 1)

        def step(start, size, carry, heads=heads):
            new = []
            for j, (m, l, acc) in zip(heads, carry):
                k = k_ref[0, kidx[j], pl.ds(start, size), :]
                v = v_ref[0, vidx[j], pl.ds(start, size), :]
                s = _dot_nt(q_ref[0, j], k)
                m_new = jnp.maximum(m, jnp.max(s, axis=-1, keepdims=True))
                alpha = jnp.exp2(m - m_new)
                p = jnp.exp2(s - m_new)
                l = alpha * l + jnp.sum(p, axis=-1, keepdims=True)
                acc = alpha * acc + _dot(p.astype(BF16), v)
                new.append((m_new, l, acc))
            return tuple(new)

        def init(j):
            if use_sink:
                m, l = jnp.full((tq, 1), sink_ref[j] * LOG2E, F32), jnp.ones((tq, 1), F32)
            else:
                m, l = jnp.full((tq, 1), NEG_BIG, F32), jnp.zeros((tq, 1), F32)
            return m, l, jnp.zeros((tq, LANES), F32)

        carry = tuple(init(j) for j in heads)
        start = 0
        for size in chunks:
            carry = step(start, size, carry)
            start += size
        even, odd = (c[2] / c[1] for c in carry)
        o_ref[0, pair] = jnp.where(left, even, odd).astype(BF16)


def _flash(q, k, v, sinks, kidx, vidx, use_sink, tq):
    b, nh, t, _ = q.shape
    nk, tkeys = k.shape[1], k.shape[2]
    nv = v.shape[1]
    return pl.pallas_call(
        functools.partial(_flash_kernel, kidx=kidx, vidx=vidx, tk=FLASH_TK, use_sink=use_sink),
        grid=(b, t // tq),
        in_specs=[pl.BlockSpec(memory_space=pltpu.SMEM),
                  pl.BlockSpec((1, nh, tq, LANES), lambda i, j: (i, 0, j, 0)),
                  pl.BlockSpec((1, nk, tkeys, LANES), lambda i, j: (i, 0, 0, 0)),
                  pl.BlockSpec((1, nv, tkeys, LANES), lambda i, j: (i, 0, 0, 0))],
        out_specs=pl.BlockSpec((1, nh // 2, tq, LANES), lambda i, j: (i, 0, j, 0)),
        out_shape=jax.ShapeDtypeStruct((b, nh // 2, t, LANES), BF16),
        compiler_params=_cparams(("parallel", "parallel")),
        name="flash_attention",
    )(sinks, q, k, v)


def _window_block(sink, q, ks, vs, has_prev, has_next, w):
    k = jnp.concatenate(ks, axis=0)
    v = jnp.concatenate(vs, axis=0)
    s = _dot_nt(q, k)
    qq = lax.broadcasted_iota(jnp.int32, s.shape, 0) & (w - 1)
    col = lax.broadcasted_iota(jnp.int32, s.shape, 1)
    never = 4 * w
    off_prev = jnp.where(has_prev, 0, never)
    off_next = jnp.where(has_next, 0, never)
    valid = ((col < w) & (col >= qq + off_prev)) | ((col >= w) & (col < 2 * w)) \
        | ((col >= 2 * w) & (col < 3 * w) & (col - 2 * w + off_next <= qq)) | (col >= 3 * w)
    s = jnp.where(valid, s, NEG_BIG)
    m = jnp.maximum(sink, jnp.max(s, axis=-1, keepdims=True))
    e = jnp.exp2(s - m)
    denom = jnp.exp2(sink - m) + jnp.sum(e, axis=-1, keepdims=True)
    return _dot(e.astype(BF16), v) / denom


def _window_kernel(sink_ref, q_ref, kp_ref, kc_ref, kn_ref, kx_ref, vp_ref, vc_ref, vn_ref, vx_ref,
                   o_ref):
    nh, w = q_ref.shape[1], kp_ref.shape[2]
    n_win = q_ref.shape[2] // w
    i, steps = pl.program_id(1), pl.num_programs(1)
    sink = jnp.concatenate([jnp.full((w, 1), sink_ref[j] * LOG2E, F32) for j in range(nh)], axis=0)
    ks = [kp_ref[0, 0]] + [kc_ref[0, 0, c * w:(c + 1) * w] for c in range(n_win)] + [kn_ref[0, 0]]
    vs = [vp_ref[0, 0]] + [vc_ref[0, 0, c * w:(c + 1) * w] for c in range(n_win)] + [vn_ref[0, 0]]
    kx, vx = kx_ref[0, 0], vx_ref[0, 0]
    left = _lane((w, LANES)) < HEAD_DIM
    for c in range(n_win):
        rows = slice(c * w, (c + 1) * w)
        has_prev = (i >= 1) if c == 0 else True
        has_next = (i <= steps - 2) if c == n_win - 1 else True
        o = _window_block(sink, q_ref[0, :, rows].reshape(nh * w, LANES), ks[c:c + 3] + [kx],
                          vs[c:c + 3] + [vx], has_prev, has_next, w)
        for p in range(nh // 2):
            o_ref[0, p, rows] = jnp.where(left, o[2 * p * w:(2 * p + 1) * w],
                                          o[(2 * p + 1) * w:(2 * p + 2) * w]).astype(BF16)


def _window_attention(q, k_all, v_all, sinks, n_lat):
    b, nh, _, _ = q.shape
    w = C_WINDOW
    nb = n_lat // w
    n_ctx = k_all.shape[2] - n_lat
    g = max(c for c in range(1, WINDOWS_PER_STEP + 1) if nb % c == 0)
    assert n_lat % n_ctx == 0 and w & (w - 1) == 0
    prev = pl.BlockSpec((1, 1, w, LANES), lambda i, j: (i, 0, jnp.maximum(g * j - 1, 0), 0))
    cur = pl.BlockSpec((1, 1, g * w, LANES), lambda i, j: (i, 0, j, 0))
    nxt = pl.BlockSpec((1, 1, w, LANES), lambda i, j: (i, 0, jnp.minimum(g * j + g, nb - 1), 0))
    ctx = pl.BlockSpec((1, 1, n_ctx, LANES), lambda i, j: (i, 0, n_lat // n_ctx, 0))
    return pl.pallas_call(
        _window_kernel,
        grid=(b, nb // g),
        in_specs=[pl.BlockSpec(memory_space=pltpu.SMEM),
                  pl.BlockSpec((1, nh, g * w, LANES), lambda i, j: (i, 0, j, 0)),
                  prev, cur, nxt, ctx, prev, cur, nxt, ctx],
        out_specs=pl.BlockSpec((1, nh // 2, g * w, LANES), lambda i, j: (i, 0, j, 0)),
        out_shape=jax.ShapeDtypeStruct((b, nh // 2, n_lat, LANES), BF16),
        compiler_params=_cparams(("parallel", "parallel")),
        name="window_attention",
    )(sinks, q, k_all, k_all, k_all, k_all, v_all, v_all, v_all, v_all)


def _mixed_residual(x_ref, oa_ref, ob_ref, oc_ref, mod, wout_ref, rows=slice(None)):
    att = jnp.concatenate([oa_ref[0, 0, rows], oa_ref[0, 1, rows], oa_ref[0, 2, rows],
                           ob_ref[0, 0, rows], ob_ref[0, 1, rows],
                           oc_ref[0, 0, rows], oc_ref[0, 1, rows], oc_ref[0, 2, rows]], axis=-1)
    return x_ref[0, rows] + mod[2:3] * _dot(att, wout_ref[...])


def _ffn_dense_kernel(x_ref, oa_ref, ob_ref, oc_ref, mod_ref, wout_ref, g_ref, w1_ref, w3_ref, w2_ref,
                      o_ref, *, n_chunks):
    mod = mod_ref[0]
    x1 = _mixed_residual(x_ref, oa_ref, ob_ref, oc_ref, mod, wout_ref)
    hf = ((_rms(x1) * g_ref[...]) * (1.0 + mod[4:5]) + mod[3:4]).astype(BF16)
    y = jnp.zeros(x1.shape, F32)
    start = 0
    for size in _tile_chunks(w1_ref.shape[1], -(-w1_ref.shape[1] // n_chunks)):
        cols = slice(start, start + size)
        a = _dot(hf, w1_ref[:, cols])
        gate = _dot(hf, w3_ref[:, cols])
        y = y + _dot((_silu(a) * gate).astype(BF16), w2_ref[cols, :])
        start += size
    o_ref[0] = x1 + mod[5:6] * y


def _resident(shape):
    return pl.BlockSpec(shape, lambda i, j: (0,) * len(shape), pipeline_mode=pl.Buffered(1))


def _ffn_dense(x, oa, ob, oc, mod, per_batch_mod, lw, tm):
    b, t, d = x.shape
    f = lw["w1"].shape[1]
    mod_map = (lambda i, j: (i, 0, 0)) if per_batch_mod else (lambda i, j: (0, 0, 0))

    def att(n):
        return pl.BlockSpec((1, n, tm, LANES), lambda i, j: (i, 0, j, 0))

    return pl.pallas_call(
        functools.partial(_ffn_dense_kernel, n_chunks=2),
        grid=(b, t // tm),
        in_specs=[pl.BlockSpec((1, tm, d), lambda i, j: (i, j, 0)), att(3), att(2), att(3),
                  pl.BlockSpec((1, 6, d), mod_map),
                  _resident((d, d)), _resident((1, d)),
                  _resident((d, f)), _resident((d, f)), _resident((f, d))],
        out_specs=pl.BlockSpec((1, tm, d), lambda i, j: (i, j, 0)),
        out_shape=jax.ShapeDtypeStruct((b, t, d), F32),
        compiler_params=_cparams(("parallel", "parallel")),
        name="ffn_dense",
    )(x, oa, ob, oc, mod, lw["w_out"], lw["g_ffn"], lw["w1"], lw["w3"], lw["w2"])


def _route_kernel(x_ref, oa_ref, ob_ref, oc_ref, mod_ref, wout_ref, g_ref, wr_ref,
                  x1_ref, hf_ref, ids_ref, gates_ref):
    half = x_ref.shape[1] // 2
    for part in range(2):
        _route_rows(slice(part * half, (part + 1) * half), x_ref, oa_ref, ob_ref, oc_ref, mod_ref,
                    wout_ref, g_ref, wr_ref, x1_ref, hf_ref, ids_ref, gates_ref)


def _route_rows(rows, x_ref, oa_ref, ob_ref, oc_ref, mod_ref, wout_ref, g_ref, wr_ref,
                x1_ref, hf_ref, ids_ref, gates_ref):
    mod = mod_ref[0]
    x1 = _mixed_residual(x_ref, oa_ref, ob_ref, oc_ref, mod, wout_ref, rows)
    hf = (_rms(x1) * g_ref[...]) * (1.0 + mod[4:5]) + mod[3:4]
    x1_ref[0, rows] = x1
    hf_ref[0, rows] = hf
    logits = _dot3(hf, wr_ref[...])
    lane = _lane(logits.shape)
    logits = jnp.where(lane < N_EXPERTS, logits, NEG_BIG)
    m1 = jnp.max(logits, axis=-1, keepdims=True)
    i1 = jnp.min(jnp.where(logits == m1, lane, LANES), axis=-1, keepdims=True)
    rest = jnp.where(lane == i1, NEG_BIG, logits)
    m2 = jnp.max(rest, axis=-1, keepdims=True)
    i2 = jnp.min(jnp.where(rest == m2, lane, LANES), axis=-1, keepdims=True)
    e2 = jnp.exp(m2 - m1)
    g1 = 1.0 / (1.0 + e2)
    ids_ref[0, rows] = jnp.where(lane == 0, i1, jnp.where(lane == 1, i2, 0))
    gates_ref[0, rows] = jnp.where(lane == 0, g1, jnp.where(lane == 1, e2 * g1, 0.0))


def _route(x, oa, ob, oc, mod, lw, tm):
    b, t, d = x.shape

    def att(n):
        return pl.BlockSpec((1, n, tm, LANES), lambda i, j: (i, 0, j, 0))

    def const(shape):
        return pl.BlockSpec(shape, lambda i, j: (0,) * len(shape))

    tok = pl.BlockSpec((1, tm, d), lambda i, j: (i, j, 0))
    meta = pl.BlockSpec((1, tm, LANES), lambda i, j: (i, j, 0))
    return pl.pallas_call(
        _route_kernel,
        grid=(b, t // tm),
        in_specs=[tok, att(3), att(2), att(3), pl.BlockSpec((1, 6, d), lambda i, j: (i, 0, 0)),
                  const((d, d)), const((1, d)), const((d, LANES))],
        out_specs=[tok, tok, meta, meta],
        out_shape=[jax.ShapeDtypeStruct((b, t, d), F32), jax.ShapeDtypeStruct((b, t, d), F32),
                   jax.ShapeDtypeStruct((b, t, LANES), jnp.int32),
                   jax.ShapeDtypeStruct((b, t, LANES), F32)],
        compiler_params=_cparams(("parallel", "parallel")),
        name="moe_route",
    )(x, oa, ob, oc, mod, lw["w_out"], lw["g_ffn"], lw["w_router"])


def _row_copy(src_ref, src_row, dst_ref, dst_row, sem):
    return pltpu.make_async_copy(src_ref.at[pl.ds(src_row, 1)], dst_ref.at[pl.ds(dst_row, 1)], sem)


def _load_indices(idx_ref, tile, idx_smem, sem):
    cp = pltpu.make_async_copy(idx_ref.at[tile], idx_smem, sem)
    cp.start()
    cp.wait()


def _wait_rows(src_ref, dst_ref, sem):
    pltpu.make_async_copy(src_ref.at[pl.ds(0, dst_ref.shape[0])], dst_ref, sem).wait()


def _experts_kernel(te_ref, tv_ref, src_ref, hf_ref, w1_ref, w3_ref, w2_ref, y_ref,
                    src_smem, xg_scr, xb_scr, acc_scr, src_sem, sems):
    i, f = pl.program_id(0), pl.program_id(1)
    tm = xb_scr.shape[0]

    def index_copy(tile):
        return pltpu.make_async_copy(src_ref.at[tile], src_smem, src_sem)

    def issue_rows(tile):
        slot = tile % 2

        def issue(r, carry):
            _row_copy(hf_ref, src_smem[r], xg_scr.at[slot], r, sems.at[slot]).start(priority=1)
            return carry

        lax.fori_loop(0, tm, issue, 0, unroll=ISSUE_UNROLL)

    nxt = jnp.minimum(i + 1, pl.num_programs(0) - 1)
    first_step = f == 0
    prefetch = first_step & (i + 1 < pl.num_programs(0)) & (tv_ref[nxt] > 0)

    @pl.when(first_step & (i == 0))
    def _():
        index_copy(0).start()
        index_copy(0).wait()
        issue_rows(0)

    @pl.when(prefetch)
    def _():
        index_copy(nxt).start()

    @pl.when(first_step & (tv_ref[i] > 0))
    def _():
        slot = i % 2
        _wait_rows(hf_ref, xg_scr.at[slot], sems.at[slot])
        xb_scr[...] = xg_scr[slot].astype(BF16)

    @pl.when(first_step)
    def _():
        acc_scr[...] = jnp.zeros(acc_scr.shape, F32)

    @pl.when(prefetch)
    def _():
        index_copy(nxt).wait()
        issue_rows(nxt)

    @pl.when(tv_ref[i] > 0)
    def _():
        xb = xb_scr[...]
        a = _dot(xb, w1_ref[...])
        gate = _dot(xb, w3_ref[...])
        acc_scr[...] += _dot((_silu(a) * gate).astype(BF16), w2_ref[...])

    @pl.when(f == pl.num_programs(1) - 1)
    def _():
        y_ref[...] = acc_scr[...]


def _experts(tile_expert, tile_valid, src, hf, lw, tf):
    n_tiles, tm = src.shape
    d = hf.shape[1]
    f = lw["w1"].shape[2]
    grid_spec = pltpu.PrefetchScalarGridSpec(
        num_scalar_prefetch=2,
        grid=(n_tiles, f // tf),
        in_specs=[pl.BlockSpec(memory_space=pltpu.VMEM), pl.BlockSpec(memory_space=pl.ANY),
                  pl.BlockSpec((None, d, tf), lambda i, k, te, tv: (te[i], 0, k)),
                  pl.BlockSpec((None, d, tf), lambda i, k, te, tv: (te[i], 0, k)),
                  pl.BlockSpec((None, tf, d), lambda i, k, te, tv: (te[i], k, 0))],
        out_specs=pl.BlockSpec((tm, d), lambda i, k, te, tv: (i, 0)),
        scratch_shapes=[pltpu.SMEM((tm,), jnp.int32), pltpu.VMEM((2, tm, d), F32),
                        pltpu.VMEM((tm, d), BF16), pltpu.VMEM((tm, d), F32),
                        pltpu.SemaphoreType.DMA(()), pltpu.SemaphoreType.DMA((2,))])
    return pl.pallas_call(
        _experts_kernel,
        grid_spec=grid_spec,
        out_shape=jax.ShapeDtypeStruct((n_tiles * tm, d), F32),
        compiler_params=_cparams(("arbitrary", "arbitrary")),
        name="moe_experts",
    )(tile_expert, tile_valid, src, hf, lw["w1"], lw["w3"], lw["w2"])


def _combine_kernel(pos_ref, y_ref, x1_ref, gates_ref, mod_ref, gfin_ref, o_ref,
                    pos_smem, y0_scr, y1_scr, pos_sem, sem):
    rows = y0_scr.shape[0]
    _load_indices(pos_ref, pl.program_id(0), pos_smem, pos_sem)

    def issue(r, carry):
        _row_copy(y_ref, pos_smem[r], y0_scr, r, sem).start(priority=0)
        _row_copy(y_ref, pos_smem[rows + r], y1_scr, r, sem).start(priority=1)
        return carry

    lax.fori_loop(0, rows, issue, 0, unroll=ISSUE_UNROLL)
    _wait_rows(y_ref, y0_scr, sem)
    _wait_rows(y_ref, y1_scr, sem)
    gates = gates_ref[...]
    y = gates[:, 0:1] * y0_scr[...] + gates[:, 1:2] * y1_scr[...]
    o_ref[...] = _rms(x1_ref[...] + mod_ref[0, 5:6] * y) * gfin_ref[...]


def _combine(pos, y, x1, gates, mod, gfin, tokens_per_batch):
    n, d = x1.shape
    n_tiles, two_rows = pos.shape
    rows = two_rows // 2
    tok = pl.BlockSpec((rows, d), lambda i: (i, 0))
    return pl.pallas_call(
        _combine_kernel,
        grid=(n_tiles,),
        in_specs=[pl.BlockSpec(memory_space=pltpu.VMEM), pl.BlockSpec(memory_space=pl.ANY), tok,
                  pl.BlockSpec((rows, LANES), lambda i: (i, 0)),
                  pl.BlockSpec((1, 6, d), lambda i: (i * rows // tokens_per_batch, 0, 0)),
                  pl.BlockSpec((1, d), lambda i: (0, 0))],
        out_specs=tok,
        out_shape=jax.ShapeDtypeStruct((n, d), F32),
        scratch_shapes=[pltpu.SMEM((two_rows,), jnp.int32), pltpu.VMEM((rows, d), F32),
                        pltpu.VMEM((rows, d), F32), pltpu.SemaphoreType.DMA(()),
                        pltpu.SemaphoreType.DMA(())],
        compiler_params=_cparams(("arbitrary",)),
        name="moe_combine",
    )(pos, y, x1, gates, mod, gfin)


def _dispatch_plan(ids, tm, rows):
    n = ids.shape[0]
    flat = ids.reshape(-1)
    onehot = (flat[None, :] == jnp.arange(N_EXPERTS)[:, None]).astype(jnp.int32)
    csum = jnp.cumsum(onehot, axis=1)
    rank = jnp.sum(csum * onehot, axis=0) - 1
    counts = csum[:, -1]
    padded = (counts + tm - 1) // tm * tm
    ends = jnp.cumsum(padded)
    pos = (ends - padded)[flat] + rank
    n_sorted = 2 * n + N_EXPERTS * tm
    starts = jnp.arange(n_sorted // tm, dtype=jnp.int32) * tm
    tile_expert = jnp.sum((starts[:, None] >= ends[None, :]).astype(jnp.int32), axis=1)
    tile_valid = (starts < ends[-1]).astype(jnp.int32)
    last_expert = tile_expert[jnp.maximum(ends[-1] // tm - 1, 0)]
    tile_expert = jnp.where(tile_valid > 0, tile_expert, last_expert)
    order = jnp.argsort(flat, stable=True).astype(jnp.int32)
    first = jnp.cumsum(counts) - counts
    row = jnp.arange(n_sorted, dtype=jnp.int32)
    row_expert = jnp.repeat(tile_expert, tm)
    r = row - (ends - padded)[row_expert]
    live = (r < counts[row_expert]) & (jnp.repeat(tile_valid, tm) > 0)
    src = jnp.where(live, order[jnp.clip(first[row_expert] + r, 0, 2 * n - 1)] // 2, 0)
    pos = pos.reshape(n // rows, rows, 2).transpose(0, 2, 1).reshape(n // rows, 2 * rows)
    return pos.astype(jnp.int32), src.reshape(n_sorted // tm, tm), tile_expert, tile_valid


def _ffn_moe(x, oa, ob, oc, mod, lw, gfin):
    b, t, d = x.shape
    n = b * t
    assert t % ROW_TILE == 0
    x1, hf, ids, gates = _route(x, oa, ob, oc, mod, lw, 512)
    pos, src, tile_expert, tile_valid = _dispatch_plan(ids.reshape(n, LANES)[:, :2], MOE_TM, ROW_TILE)
    y = _experts(tile_expert, tile_valid, src, hf.reshape(n, d), lw, MOE_TF)
    out = _combine(pos, y, x1.reshape(n, d), gates.reshape(n, LANES), mod, gfin, t)
    return out.reshape(b, t, d)


def _rope_tables(n_lat):
    rows = n_lat // GRID_W
    row = jnp.repeat(jnp.arange(rows, dtype=F32), GRID_W)
    col = jnp.tile(jnp.arange(GRID_W, dtype=F32), rows)

    def table(rot_dim):
        nf = rot_dim // 4
        inv = ROPE_THETA ** (-jnp.arange(nf, dtype=F32) / nf)
        ang_r, ang_c = row[:, None] * inv, col[:, None] * inv
        ang = jnp.concatenate([ang_r, ang_r, ang_c, ang_c], axis=-1)
        sign = jnp.tile(jnp.concatenate([-jnp.ones(nf, F32), jnp.ones(nf, F32)]), 2)
        return jnp.cos(ang), jnp.sin(ang) * sign

    cos_h, sin_h = table(HEAD_DIM)
    cos_r, sin_r = table(B_ROPE)
    one = jnp.ones((n_lat, B_NOPE), F32)
    pad = LANES - B_NOPE - B_ROPE
    cos_b = jnp.concatenate([one, cos_r, jnp.ones((n_lat, pad), F32)], axis=-1)
    sin_b = jnp.concatenate([0 * one, sin_r, jnp.zeros((n_lat, pad), F32)], axis=-1)
    return (jnp.tile(cos_h, (1, 2)), jnp.tile(sin_h, (1, 2)), cos_b, sin_b)


def _permute_heads(w, axis):
    shape = w.shape
    w = w.reshape(shape[:axis] + (6, HEAD_DIM) + shape[axis + 1:])
    w = jnp.take(w, jnp.array(HEAD_PERM), axis=axis)
    return w.reshape(shape)


def _layer_weights(i, w_in, norm_ffn, a_q_norm, a_k_norm, b_q_norm, b_kv_norm, w_uq, w_ukv, w_out):
    d = w_in.shape[1]
    wi = w_in[i]
    qk = A_HEADS * HEAD_DIM
    kv = A_KV * HEAD_DIM
    b0 = A_COLS
    c0 = A_COLS + B_COLS
    k_rope = jnp.pad(wi[:, b0 + B_Q_RANK + B_KV_RANK:c0], ((0, 0), (B_NOPE, LANES - B_NOPE - B_ROPE)))
    w_in_p = jnp.concatenate([
        _permute_heads(wi[:, :qk], 1), wi[:, qk:A_COLS],
        wi[:, b0:b0 + B_Q_RANK + B_KV_RANK], k_rope,
        _permute_heads(wi[:, c0:c0 + qk], 1), wi[:, c0 + qk:]], axis=1).astype(BF16)
    w_uq_p = jnp.pad(w_uq[i].reshape(B_Q_RANK, B_HEADS, B_NOPE + B_ROPE),
                     ((0, 0), (0, 0), (0, LANES - B_NOPE - B_ROPE))).reshape(B_Q_RANK, B_HEADS * LANES)
    ukv = w_ukv[i].reshape(B_KV_RANK, B_HEADS, B_NOPE + B_V)
    k_up = jnp.pad(ukv[:, :, :B_NOPE], ((0, 0), (0, 0), (0, LANES - B_NOPE))).reshape(B_KV_RANK, -1)
    v_up = ukv[:, :, B_NOPE:].reshape(B_KV_RANK, -1)
    wo = w_out[i]
    w_out_p = jnp.concatenate([_permute_heads(wo[:qk], 0), wo[qk:qk + B_HEADS * B_V],
                               _permute_heads(wo[qk + B_HEADS * B_V:], 0)], axis=0).astype(BF16)
    blk = (jnp.arange(LANES)[:, None] // HEAD_DIM) == (jnp.arange(LANES)[None, :] // HEAD_DIM)
    del kv, d
    return {
        "w_in": w_in_p,
        "m2": (blk.astype(F32) / HEAD_DIM).astype(BF16),
        "a_q_gain": jnp.tile(a_q_norm[i], 2)[None], "a_k_gain": jnp.tile(a_k_norm[i], 2)[None],
        "b_q_gain": b_q_norm[i][None], "b_kv_gain": b_kv_norm[i][None],
        "w_uq": w_uq_p.astype(BF16),
        "w_ukv": jnp.concatenate([k_up, v_up], axis=1).astype(BF16),
        "w_out": w_out_p,
        "g_ffn": norm_ffn[i][None],
    }


def kernel(x, c, ctx, c_ctx, w_mod, b_mod, norm_mix, norm_ffn, w_in, a_q_norm, a_k_norm, b_q_norm,
           b_kv_norm, w_uq, w_ukv, c_sinks, w_out, w1_dense, w3_dense, w2_dense, w_router, w1_moe,
           w3_moe, w2_moe, final_norm):
    bsz, n_lat, d = x.shape
    n_ctx = ctx.shape[1]
    depth = w_mod.shape[0]
    mod_rows = 16
    cc = jnp.concatenate([c, c_ctx[None], jnp.zeros((mod_rows - bsz - 1, d), F32)], axis=0)
    mods = _modulation(cc, w_mod, b_mod)
    tables = _rope_tables(n_lat)
    no_sink = jnp.zeros((A_HEADS,), F32)
    xc = ctx
    for i in range(depth):
        last = i == depth - 1
        mod_lat = mods[i, :bsz].reshape(bsz, 6, d)
        mod_ctx = mods[i, bsz:bsz + 1].reshape(1, 6, d)
        lw = _layer_weights(i, w_in, norm_ffn, a_q_norm, a_k_norm, b_q_norm, b_kv_norm, w_uq, w_ukv,
                            w_out)
        g_mix = norm_mix[i][None]
        lat = _in_projection(x, mod_lat, True, g_mix, lw, tables, True, 512)
        cx = _in_projection(xc, mod_ctx, False, g_mix, lw, tables, False, n_ctx)
        qa, ka, va, qb, kb, vb, qc, kc, vc = lat
        cqa, cka, cva, cqb, ckb, cvb, cqc, ckc, cvc = cx

        def both(u, w):
            return jnp.concatenate([u, w], axis=2)

        sinks = jnp.take(c_sinks[i], jnp.array(HEAD_PERM))
        oa = _flash(qa, both(ka, cka), both(va, cva), no_sink, (0,) * 6, (0,) * 6, False, 512)
        ob = _flash(qb, both(kb, ckb), both(vb, cvb), no_sink, (0, 1, 2, 3), (0, 0, 1, 1), False, 512)
        oc = _window_attention(qc, both(kc, ckc), both(vc, cvc), sinks, n_lat)
        if not last:
            coa = _flash(cqa, cka, cva, no_sink, (0,) * 6, (0,) * 6, False, n_ctx)
            cob = _flash(cqb, ckb, cvb, no_sink, (0, 1, 2, 3), (0, 0, 1, 1), False, n_ctx)
            coc = _flash(cqc, ckc, cvc, sinks, (0,) * 6, (0,) * 6, True, n_ctx)
        j = i // 2
        if i % 2 == 0:
            lw.update(w1=w1_dense[j].astype(BF16), w3=w3_dense[j].astype(BF16),
                      w2=w2_dense[j].astype(BF16))
            x_new = _ffn_dense(x, oa, ob, oc, mod_lat, True, lw, 512)
            if not last:
                xc = _ffn_dense(xc, coa, cob, coc, mod_ctx, False, lw, n_ctx)
            x = x_new
        else:
            assert last, "the expert channel mixer is fused with the final norm"
            lw.update(w1=w1_moe[j].astype(BF16), w3=w3_moe[j].astype(BF16), w2=w2_moe[j].astype(BF16),
                      w_router=jnp.pad(w_router[j], ((0, 0), (0, LANES - N_EXPERTS))))
            x = _ffn_moe(x, oa, ob, oc, mod_lat, lw, final_norm[None])
    return x
```
